```python
import math
import jax, jax.numpy as jnp
from jax import lax
import numpy as np

D_MODEL = 1024
BATCH = 4
SEQ = 4096
DEPTH = 1

D_MIX = D_MODEL
D_POOL = D_MIX // 2
D_ATTN = D_MIX - D_POOL
POOL_WINDOWS = (2, 4, 8, 16)
N_POOL_GROUPS = len(POOL_WINDOWS)
POOL_GROUP_DIM = D_POOL // N_POOL_GROUPS
HEAD_DIM = 64
N_HEADS = D_ATTN // HEAD_DIM
Q_BLOCK = 128
D_FF = 4 * D_MODEL
D_IN_PROJ = D_POOL + 3 * D_ATTN
EPS = 1e-6

kernel_name = "hymba_pool_stickbreak_block"


def rmsnorm(x, g):
    xf = x.astype(jnp.float32)
    r = lax.rsqrt(jnp.mean(xf * xf, axis=-1, keepdims=True) + EPS)
    return (xf * r * g.astype(jnp.float32)).astype(x.dtype)


def pool_mixer(u, pool_w, pool_scale):
    B, S, _ = u.shape
    ug = u.reshape(B, S, N_POOL_GROUPS, POOL_GROUP_DIM)
    pos = jnp.arange(S, dtype=jnp.int32)
    outs = []
    for g, w in enumerate(POOL_WINDOWS):
        xg = ug[:, :, g, :].astype(jnp.float32)
        cs = jnp.cumsum(xg, axis=1)
        cs_lag = jnp.pad(cs, ((0, 0), (w, 0), (0, 0)))[:, :S]
        count = jnp.minimum(pos + 1, w).astype(jnp.float32)[None, :, None]
        mean = (cs - cs_lag) / count
        outs.append(mean - xg)
    pooled = jnp.stack(outs, axis=2)
    mapped = jnp.einsum('bsgc,gcd->bsgd', pooled, pool_w.astype(jnp.float32))
    y = mapped.reshape(B, S, D_POOL) * pool_scale.astype(jnp.float32)
    return y.astype(u.dtype)


def stick_breaking_attention(q, k, v):
    B, H, S, Dh = q.shape
    scale = 1.0 / math.sqrt(Dh)
    n_blocks = S // Q_BLOCK
    outs = []
    for i in range(n_blocks):
        q0, end = i * Q_BLOCK, (i + 1) * Q_BLOCK
        qb = q[:, :, q0:end]
        kb = k[:, :, :end]
        vb = v[:, :, :end]
        z = jnp.einsum('bhqd,bhkd->bhqk', qb, kb).astype(jnp.float32) * scale
        t_idx = q0 + jnp.arange(Q_BLOCK, dtype=jnp.int32)
        s_idx = jnp.arange(end, dtype=jnp.int32)
        mask = s_idx[None, :] < t_idx[:, None]
        log1m = jnp.where(mask, jax.nn.log_sigmoid(-z), 0.0)
        tail = lax.cumsum(log1m, axis=3, reverse=True) - log1m
        log_a = jax.nn.log_sigmoid(z) + tail
        a = jnp.where(mask, jnp.exp(log_a), 0.0)
        ob = jnp.einsum('bhqk,bhkd->bhqd', a, vb.astype(jnp.float32))
        outs.append(ob)
    o = jnp.concatenate(outs, axis=2)
    return o.astype(q.dtype)


def setup_inputs(seed: int = 0) -> dict:
    key = jax.random.key(seed)
    ks = jax.random.split(key, 12)
    f32 = jnp.float32
    x = jax.random.normal(ks[0], (BATCH, SEQ, D_MODEL), f32)
    norm1_g = 1.0 + 0.02 * jax.random.normal(ks[1], (D_MODEL,), f32)
    w_in = jax.random.normal(ks[2], (D_MODEL, D_IN_PROJ), f32) * D_MODEL ** -0.5
    pool_w = jax.random.normal(ks[3], (N_POOL_GROUPS, POOL_GROUP_DIM, POOL_GROUP_DIM), f32) * POOL_GROUP_DIM ** -0.5
    pool_scale = 0.5 + 0.02 * jax.random.normal(ks[4], (D_POOL,), f32)
    pool_out_g = 1.0 + 0.02 * jax.random.normal(ks[5], (D_POOL,), f32)
    attn_out_g = 1.0 + 0.02 * jax.random.normal(ks[6], (D_ATTN,), f32)
    w_out = jax.random.normal(ks[7], (D_MIX, D_MODEL), f32) * D_MIX ** -0.5
    norm2_g = 1.0 + 0.02 * jax.random.normal(ks[8], (D_MODEL,), f32)
    w_up = jax.random.normal(ks[9], (D_MODEL, D_FF), f32) * D_MODEL ** -0.5
    w_down = jax.random.normal(ks[10], (D_FF, D_MODEL), f32) * D_FF ** -0.5
    final_g = 1.0 + 0.02 * jax.random.normal(ks[11], (D_MODEL,), f32)
    return {"x": x, "norm1_g": norm1_g, "w_in": w_in, "pool_w": pool_w,
            "pool_scale": pool_scale, "pool_out_g": pool_out_g, "attn_out_g": attn_out_g,
            "w_out": w_out, "norm2_g": norm2_g, "w_up": w_up, "w_down": w_down,
            "final_g": final_g}


def reference(x, norm1_g, w_in, pool_w, pool_scale, pool_out_g, attn_out_g,
              w_out, norm2_g, w_up, w_down, final_g):
    B, S, _ = x.shape
    h = x
    for _ in range(DEPTH):
        hn = rmsnorm(h, norm1_g)
        proj = jnp.einsum('bsd,de->bse', hn, w_in)
        u_pool = proj[..., :D_POOL]
        q = proj[..., D_POOL:D_POOL + D_ATTN]
        k = proj[..., D_POOL + D_ATTN:D_POOL + 2 * D_ATTN]
        v = proj[..., D_POOL + 2 * D_ATTN:]
        to_heads = lambda t: t.reshape(B, S, N_HEADS, HEAD_DIM).transpose(0, 2, 1, 3)
        y_pool = pool_mixer(u_pool, pool_w, pool_scale)
        o = stick_breaking_attention(to_heads(q), to_heads(k), to_heads(v))
        y_attn = o.transpose(0, 2, 1, 3).reshape(B, S, D_ATTN)
        mixed = jnp.concatenate([rmsnorm(y_pool, pool_out_g),
                                 rmsnorm(y_attn, attn_out_g)], axis=-1)
        h = h + jnp.einsum('bse,ed->bsd', mixed, w_out)
        hn2 = rmsnorm(h, norm2_g)
        up = jnp.einsum('bsd,df->bsf', hn2, w_up)
        act = jnp.square(jax.nn.relu(up))
        h = h + jnp.einsum('bsf,fd->bsd', act, w_down)
    return rmsnorm(h, final_g)
```

```python
import functools

import jax
import jax.numpy as jnp
from jax import lax
from jax.experimental import pallas as pl
from jax.experimental.pallas import tpu as pltpu

EPS = 1e-6
POOL_WINDOWS = (2, 4, 8, 16)
HEAD_DIM = 64
LANES = 128
POOL_HALO = 16
VMEM_LIMIT_BYTES = 56 * 1024 * 1024

BF16 = jnp.bfloat16
F32 = jnp.float32


def _rms(x, g):
    r = lax.rsqrt(jnp.mean(x * x, axis=-1, keepdims=True) + EPS)
    return x * r * g


def _in_proj_pool_kernel(x_ref, g1_ref, w_in_ref, pool_w_ref, pool_scale_ref, pool_g_ref,
                         pool_out_ref, q_ref, k_ref, v_ref, ubuf_ref, *, tm, d_pool, d_attn):
    s = pl.program_id(1)
    gdim = d_pool // len(POOL_WINDOWS)

    hn = _rms(x_ref[0], g1_ref[...]).astype(BF16)
    proj = jnp.dot(hn, w_in_ref[...], preferred_element_type=F32)

    u = proj[:, :d_pool]
    q_ref[0] = (proj[:, d_pool:d_pool + d_attn] * (HEAD_DIM ** -0.5)).astype(BF16)
    k_ref[0] = proj[:, d_pool + d_attn:d_pool + 2 * d_attn].astype(BF16)
    v_ref[0] = proj[:, d_pool + 2 * d_attn:].astype(BF16)

    @pl.when(s == 0)
    def _():
        ubuf_ref[0:POOL_HALO, :] = jnp.zeros((POOL_HALO, d_pool), F32)

    ubuf_ref[POOL_HALO:POOL_HALO + tm, :] = u

    pos = s * tm + lax.broadcasted_iota(jnp.int32, (tm, 1), 0)
    mapped = []
    for g, w in enumerate(POOL_WINDOWS):
        lo, hi = g * gdim, (g + 1) * gdim
        win = ubuf_ref[POOL_HALO:POOL_HALO + tm, lo:hi]
        for j in range(1, w):
            win = win + ubuf_ref[POOL_HALO - j:POOL_HALO - j + tm, lo:hi]
        count = jnp.minimum(pos + 1, w).astype(F32)
        pooled = win / count - u[:, lo:hi]
        mapped.append(jnp.dot(pooled.astype(BF16), pool_w_ref[g], preferred_element_type=F32))
    y = jnp.concatenate(mapped, axis=-1) * pool_scale_ref[...]
    pool_out_ref[0] = _rms(y, pool_g_ref[...]).astype(BF16)

    ubuf_ref[0:POOL_HALO, :] = ubuf_ref[tm:tm + POOL_HALO, :]


def _in_proj_pool(x, norm1_g, w_in, pool_w, pool_scale, pool_out_g, *, tm):
    B, S, D = x.shape
    n_groups, gdim, _ = pool_w.shape
    d_pool = n_groups * gdim
    d_attn = (w_in.shape[1] - d_pool) // 3
    kern = functools.partial(_in_proj_pool_kernel, tm=tm, d_pool=d_pool, d_attn=d_attn)
    const2 = lambda b, s: (0, 0)
    tok = lambda b, s: (b, s, 0)
    return pl.pallas_call(
        kern,
        grid=(B, S // tm),
        in_specs=[
            pl.BlockSpec((1, tm, D), tok),
            pl.BlockSpec((1, D), const2),
            pl.BlockSpec(w_in.shape, const2),
            pl.BlockSpec(pool_w.shape, lambda b, s: (0, 0, 0)),
            pl.BlockSpec((1, d_pool), const2),
            pl.BlockSpec((1, d_pool), const2),
        ],
        out_specs=[
            pl.BlockSpec((1, tm, d_pool), tok),
            pl.BlockSpec((1, tm, d_attn), tok),
            pl.BlockSpec((1, tm, d_attn), tok),
            pl.BlockSpec((1, tm, d_attn), tok),
        ],
        out_shape=[
            jax.ShapeDtypeStruct((B, S, d_pool), BF16),
            jax.ShapeDtypeStruct((B, S, d_attn), BF16),
            jax.ShapeDtypeStruct((B, S, d_attn), BF16),
            jax.ShapeDtypeStruct((B, S, d_attn), BF16),
        ],
        scratch_shapes=[pltpu.VMEM((POOL_HALO + tm, d_pool), F32)],
        compiler_params=pltpu.CompilerParams(
            dimension_semantics=("arbitrary", "arbitrary"),
            vmem_limit_bytes=VMEM_LIMIT_BYTES),
        name="in_proj_pool",
    )(x, norm1_g.reshape(1, D), w_in, pool_w, pool_scale.reshape(1, d_pool),
      pool_out_g.reshape(1, d_pool))


def _stickbreak_kernel(q_ref, k_ref, v_ref, o_ref, *, t):
    i = pl.program_id(2)
    lane = lax.broadcasted_iota(jnp.int32, (t, LANES), 1)
    row = lax.broadcasted_iota(jnp.int32, (t, t), 0)
    col = lax.broadcasted_iota(jnp.int32, (t, t), 1)
    suffix = (row > col).astype(BF16)
    causal = col < row

    q2 = q_ref[0]
    outs = []
    for h in range(LANES // HEAD_DIM):
        in_head = (lane >= h * HEAD_DIM) & (lane < (h + 1) * HEAD_DIM)
        qh = jnp.where(in_head, q2, jnp.zeros_like(q2))

        def block(j, carry, acc, masked):
            kb = k_ref[0, pl.ds(j * t, t), :]
            vb = v_ref[0, pl.ds(j * t, t), :]
            z = lax.dot_general(qh, kb, (((1,), (1,)), ((), ())), preferred_element_type=F32)
            lse = jnp.log(1.0 + jnp.exp(-jnp.abs(z)))
            log1m = jnp.minimum(-z, 0.0) - lse
            log_beta = jnp.minimum(z, 0.0) - lse
            if masked:
                log1m = jnp.where(causal, log1m, 0.0)
            tail = jnp.dot(log1m.astype(BF16), suffix, preferred_element_type=F32)
            a = jnp.exp(log_beta + tail)
            if masked:
                a = jnp.where(causal, a, 0.0)
            pv = jnp.dot(a.astype(BF16), vb, preferred_element_type=F32)
            acc = acc + jnp.exp(carry) * pv
            carry = carry + tail[:, 0:1] + log1m[:, 0:1]
            return carry, acc

        carry0 = jnp.zeros((t, 1), F32)
        acc0 = jnp.zeros((t, LANES), F32)
        carry, acc = block(i, carry0, acc0, True)

        def body(n, state):
            return block(i - 1 - n, *state, False)

        carry, acc = lax.fori_loop(0, i, body, (carry, acc))
        outs.append(acc)

    o_ref[0] = jnp.where(lane < HEAD_DIM, outs[0], outs[1])


def _stickbreak(q, k, v, *, t):
    B, S, d_attn = q.shape
    kern = functools.partial(_stickbreak_kernel, t=t)
    blk = lambda b, p, i: (b, i, p)
    full = lambda b, p, i: (b, 0, p)
    return pl.pallas_call(
        kern,
        grid=(B, d_attn // LANES, S // t),
        in_specs=[
            pl.BlockSpec((1, t, LANES), blk),
            pl.BlockSpec((1, S, LANES), full),
            pl.BlockSpec((1, S, LANES), full),
        ],
        out_specs=pl.BlockSpec((1, t, LANES), blk),
        out_shape=jax.ShapeDtypeStruct((B, S, d_attn), F32),
        compiler_params=pltpu.CompilerParams(
            dimension_semantics=("arbitrary", "arbitrary", "arbitrary"),
            vmem_limit_bytes=VMEM_LIMIT_BYTES),
        name="stickbreak",
    )(q, k, v)


def _out_mlp_kernel(x_ref, pool_ref, attn_ref, attn_g_ref, w_out_ref, g2_ref, w_up_ref, w_down_ref,
                    gf_ref, o_ref, *, ff_chunk):
    attn_n = _rms(attn_ref[...], attn_g_ref[...]).astype(BF16)
    mixed = jnp.concatenate([pool_ref[...], attn_n], axis=-1)
    h = x_ref[...] + jnp.dot(mixed, w_out_ref[...], preferred_element_type=F32)

    hn2 = _rms(h, g2_ref[...]).astype(BF16)
    d_ff = w_up_ref.shape[1]
    acc = h
    for c in range(d_ff // ff_chunk):
        up = jnp.dot(hn2, w_up_ref[:, c * ff_chunk:(c + 1) * ff_chunk], preferred_element_type=F32)
        act = jnp.square(jnp.maximum(up, 0.0)).astype(BF16)
        acc = acc + jnp.dot(act, w_down_ref[c * ff_chunk:(c + 1) * ff_chunk, :],
                            preferred_element_type=F32)
    o_ref[...] = _rms(acc, gf_ref[...])


def _out_mlp(x2, pool_n, y_attn, attn_out_g, w_out, norm2_g, w_up, w_down, final_g, *, tm, ff_chunk):
    T, D = x2.shape
    d_pool = pool_n.shape[1]
    d_attn = y_attn.shape[1]
    d_ff = w_up.shape[1]
    kern = functools.partial(_out_mlp_kernel, ff_chunk=ff_chunk)
    tok = lambda i: (i, 0)
    const = lambda i: (0, 0)
    resident = functools.partial(pl.BlockSpec, index_map=const, pipeline_mode=pl.Buffered(1))
    return pl.pallas_call(
        kern,
        grid=(T // tm,),
        in_specs=[
            pl.BlockSpec((tm, D), tok),
            pl.BlockSpec((tm, d_pool), tok),
            pl.BlockSpec((tm, d_attn), tok),
            pl.BlockSpec((1, d_attn), const),
            resident((d_pool + d_attn, D)),
            pl.BlockSpec((1, D), const),
            resident((D, d_ff)),
            resident((d_ff, D)),
            pl.BlockSpec((1, D), const),
        ],
        out_specs=pl.BlockSpec((tm, D), tok),
        out_shape=jax.ShapeDtypeStruct((T, D), F32),
        compiler_params=pltpu.CompilerParams(
            dimension_semantics=("arbitrary",),
            vmem_limit_bytes=VMEM_LIMIT_BYTES),
        name="out_mlp",
    )(x2, pool_n, y_attn, attn_out_g.reshape(1, d_attn), w_out, norm2_g.reshape(1, D), w_up, w_down,
      final_g.reshape(1, D))


def kernel(x, norm1_g, w_in, pool_w, pool_scale, pool_out_g, attn_out_g, w_out, norm2_g, w_up, w_down,
           final_g):
    B, S, D = x.shape
    pool_n, q, k, v = _in_proj_pool(x, norm1_g, w_in.astype(BF16), pool_w.astype(BF16), pool_scale,
                                    pool_out_g, tm=512)
    y_attn = _stickbreak(q, k, v, t=256)
    d_pool = pool_n.shape[-1]
    out = _out_mlp(x.reshape(B * S, D), pool_n.reshape(B * S, d_pool), y_attn.reshape(B * S, -1),
                   attn_out_g, w_out.astype(BF16), norm2_g, w_up.astype(BF16), w_down.astype(BF16),
                   final_g, tm=512, ff_chunk=1024)
    return out.reshape(B, S, D)
```

```python
import functools
import math

import jax
import jax.numpy as jnp
from jax import lax
from jax.experimental import pallas as pl
from jax.experimental.pallas import tpu as pltpu

EPS = 1e-6
LOG2_E = math.log2(math.e)
POOL_WINDOWS = (2, 4, 8, 16)
HEAD_DIM = 64
LANES = 128
EXIT_LOG2 = 152.0
POOL_HALO = 16
VMEM_LIMIT_BYTES = 56 * 1024 * 1024

BF16 = jnp.bfloat16
F32 = jnp.float32


def _rms(x, g):
    r = lax.rsqrt(jnp.mean(x * x, axis=-1, keepdims=True) + EPS)
    return x * r * g


def _in_proj_pool_kernel(x_ref, g1_ref, w_in_ref, pool_w_ref, pool_scale_ref, pool_g_ref,
                         pool_out_ref, q_ref, k_ref, v_ref, ubuf_ref, *, tm, d_pool, d_attn):
    s = pl.program_id(1)
    gdim = d_pool // len(POOL_WINDOWS)

    hn = _rms(x_ref[0], g1_ref[...]).astype(BF16)
    proj = jnp.dot(hn, w_in_ref[...], preferred_element_type=F32)

    u = proj[:, :d_pool]
    q_ref[0] = (proj[:, d_pool:d_pool + d_attn] * (LOG2_E * HEAD_DIM ** -0.5)).astype(BF16)
    k_ref[0] = proj[:, d_pool + d_attn:d_pool + 2 * d_attn].astype(BF16)
    v_ref[0] = proj[:, d_pool + 2 * d_attn:].astype(BF16)

    @pl.when(s == 0)
    def _():
        ubuf_ref[0:POOL_HALO, :] = jnp.zeros((POOL_HALO, d_pool), F32)

    ubuf_ref[POOL_HALO:POOL_HALO + tm, :] = u

    pos = s * tm + lax.broadcasted_iota(jnp.int32, (tm, 1), 0)
    mapped = []
    for g, w in enumerate(POOL_WINDOWS):
        lo, hi = g * gdim, (g + 1) * gdim
        win = ubuf_ref[POOL_HALO:POOL_HALO + tm, lo:hi]
        for j in range(1, w):
            win = win + ubuf_ref[POOL_HALO - j:POOL_HALO - j + tm, lo:hi]
        count = jnp.minimum(pos + 1, w).astype(F32)
        pooled = win / count - u[:, lo:hi]
        mapped.append(jnp.dot(pooled.astype(BF16), pool_w_ref[g], preferred_element_type=F32))
    y = jnp.concatenate(mapped, axis=-1) * pool_scale_ref[...]
    pool_out_ref[0] = _rms(y, pool_g_ref[...]).astype(BF16)

    ubuf_ref[0:POOL_HALO, :] = ubuf_ref[tm:tm + POOL_HALO, :]


def _in_proj_pool(x, norm1_g, w_in, pool_w, pool_scale, pool_out_g, *, tm):
    B, S, D = x.shape
    n_groups, gdim, _ = pool_w.shape
    d_pool = n_groups * gdim
    d_attn = (w_in.shape[1] - d_pool) // 3
    kern = functools.partial(_in_proj_pool_kernel, tm=tm, d_pool=d_pool, d_attn=d_attn)
    const2 = lambda b, s: (0, 0)
    tok = lambda b, s: (b, s, 0)
    return pl.pallas_call(
        kern,
        grid=(B, S // tm),
        in_specs=[
            pl.BlockSpec((1, tm, D), tok),
            pl.BlockSpec((1, D), const2),
            pl.BlockSpec(w_in.shape, const2),
            pl.BlockSpec(pool_w.shape, lambda b, s: (0, 0, 0)),
            pl.BlockSpec((1, d_pool), const2),
            pl.BlockSpec((1, d_pool), const2),
        ],
        out_specs=[
            pl.BlockSpec((1, tm, d_pool), tok),
            pl.BlockSpec((1, tm, d_attn), tok),
            pl.BlockSpec((1, tm, d_attn), tok),
            pl.BlockSpec((1, tm, d_attn), tok),
        ],
        out_shape=[
            jax.ShapeDtypeStruct((B, S, d_pool), BF16),
            jax.ShapeDtypeStruct((B, S, d_attn), BF16),
            jax.ShapeDtypeStruct((B, S, d_attn), BF16),
            jax.ShapeDtypeStruct((B, S, d_attn), BF16),
        ],
        scratch_shapes=[pltpu.VMEM((POOL_HALO + tm, d_pool), F32)],
        compiler_params=pltpu.CompilerParams(
            dimension_semantics=("arbitrary", "arbitrary"),
            vmem_limit_bytes=VMEM_LIMIT_BYTES),
        name="in_proj_pool",
    )(x, norm1_g.reshape(1, D), w_in, pool_w, pool_scale.reshape(1, d_pool),
      pool_out_g.reshape(1, d_pool))


def _stickbreak_kernel(q_ref, k_ref, v_ref, o_ref, *, t):
    i = pl.program_id(2)
    n_heads = LANES // HEAD_DIM
    rows = n_heads * t
    lane = lax.broadcasted_iota(jnp.int32, (t, LANES), 1)
    key_r = lax.broadcasted_iota(jnp.int32, (t, t), 0)
    key_c = lax.broadcasted_iota(jnp.int32, (t, t), 1)
    suffix = (key_r > key_c).astype(BF16)
    qrow = lax.broadcasted_iota(jnp.int32, (rows, t), 0) & (t - 1)
    causal = lax.broadcasted_iota(jnp.int32, (rows, t), 1) < qrow

    q2 = q_ref[0]
    qs = jnp.concatenate(
        [jnp.where((lane >= h * HEAD_DIM) & (lane < (h + 1) * HEAD_DIM), q2, jnp.zeros_like(q2))
         for h in range(n_heads)], axis=0)

    def blocks(js, diag_first):
        sps, log_betas = [], []
        for n, j in enumerate(js):
            kb = k_ref[0, pl.ds(j * t, t), :]
            z = lax.dot_general(qs, kb, (((1,), (1,)), ((), ())), preferred_element_type=F32)
            sp = jnp.maximum(z, 0.0) + jnp.log2(1.0 + jnp.exp2(-jnp.abs(z)))
            log_betas.append(z - sp)
            if diag_first and n == 0:
                sp = jnp.where(causal, sp, 0.0)
            sps.append(sp)
        sp_all = jnp.concatenate(sps, axis=0)
        tail_all = jnp.dot(sp_all.astype(BF16), suffix, preferred_element_type=F32)
        out = []
        for n, j in enumerate(js):
            tail = tail_all[n * rows:(n + 1) * rows]
            a = jnp.exp2(log_betas[n] - tail)
            if diag_first and n == 0:
                a = jnp.where(causal, a, 0.0)
            vb = v_ref[0, pl.ds(j * t, t), :]
            pv = jnp.dot(a.astype(BF16), vb, preferred_element_type=F32)
            out.append((pv, tail[:, 0:1] + sps[n][:, 0:1]))
        return out

    def add_pair(carry, acc, j0, j1, has_second, diag_first):
        (pv0, rs0), (pv1, rs1) = blocks([j0, j1], diag_first)
        acc = acc + jnp.exp2(-carry) * pv0
        carry = carry + rs0
        acc = acc + jnp.where(has_second, jnp.exp2(-carry), 0.0) * pv1
        carry = carry + jnp.where(has_second, rs1, 0.0)
        return carry, acc

    carry = jnp.zeros((rows, 1), F32)
    acc = jnp.zeros((rows, LANES), F32)
    carry, acc = add_pair(carry, acc, i, jnp.maximum(i - 1, 0), i >= 1, True)

    def more_blocks(state):
        n, min_carry, _, _ = state
        return (n < i // 2) & (min_carry < EXIT_LOG2)

    def pair_body(state):
        n, _, carry, acc = state
        j0 = i - 2 - 2 * n
        carry, acc = add_pair(carry, acc, j0, jnp.maximum(j0 - 1, 0), j0 >= 1, False)
        return n + 1, jnp.min(carry), carry, acc

    _, _, carry, acc = lax.while_loop(more_blocks, pair_body, (0, jnp.min(carry), carry, acc))
    o_ref[0] = jnp.where(lane < HEAD_DIM, acc[0:t], acc[t:2 * t])


def _stickbreak(q, k, v, *, t):
    B, S, d_attn = q.shape
    kern = functools.partial(_stickbreak_kernel, t=t)
    blk = lambda b, p, i: (b, i, p)
    full = lambda b, p, i: (b, 0, p)
    return pl.pallas_call(
        kern,
        grid=(B, d_attn // LANES, S // t),
        in_specs=[
            pl.BlockSpec((1, t, LANES), blk),
            pl.BlockSpec((1, S, LANES), full),
            pl.BlockSpec((1, S, LANES), full),
        ],
        out_specs=pl.BlockSpec((1, t, LANES), blk),
        out_shape=jax.ShapeDtypeStruct((B, S, d_attn), F32),
        compiler_params=pltpu.CompilerParams(
            dimension_semantics=("arbitrary", "arbitrary", "arbitrary"),
            vmem_limit_bytes=VMEM_LIMIT_BYTES),
        name="stickbreak",
    )(q, k, v)


def _out_mlp_kernel(x_ref, pool_ref, attn_ref, attn_g_ref, w_out_ref, g2_ref, w_up_ref, w_down_ref,
                    gf_ref, o_ref, *, ff_chunk):
    attn_n = _rms(attn_ref[...], attn_g_ref[...]).astype(BF16)
    mixed = jnp.concatenate([pool_ref[...], attn_n], axis=-1)
    h = x_ref[...] + jnp.dot(mixed, w_out_ref[...], preferred_element_type=F32)

    hn2 = _rms(h, g2_ref[...]).astype(BF16)
    d_ff = w_up_ref.shape[1]
    acc = h
    for c in range(d_ff // ff_chunk):
        up = jnp.dot(hn2, w_up_ref[:, c * ff_chunk:(c + 1) * ff_chunk], preferred_element_type=F32)
        act = jnp.square(jnp.maximum(up, 0.0)).astype(BF16)
        acc = acc + jnp.dot(act, w_down_ref[c * ff_chunk:(c + 1) * ff_chunk, :],
                            preferred_element_type=F32)
    o_ref[...] = _rms(acc, gf_ref[...])


def _out_mlp(x2, pool_n, y_attn, attn_out_g, w_out, norm2_g, w_up, w_down, final_g, *, tm, ff_chunk):
    T, D = x2.shape
    d_pool = pool_n.shape[1]
    d_attn = y_attn.shape[1]
    d_ff = w_up.shape[1]
    kern = functools.partial(_out_mlp_kernel, ff_chunk=ff_chunk)
    tok = lambda i: (i, 0)
    const = lambda i: (0, 0)
    resident = functools.partial(pl.BlockSpec, index_map=const, pipeline_mode=pl.Buffered(1))
    return pl.pallas_call(
        kern,
        grid=(T // tm,),
        in_specs=[
            pl.BlockSpec((tm, D), tok),
            pl.BlockSpec((tm, d_pool), tok),
            pl.BlockSpec((tm, d_attn), tok),
            pl.BlockSpec((1, d_attn), const),
            resident((d_pool + d_attn, D)),
            pl.BlockSpec((1, D), const),
            resident((D, d_ff)),
            resident((d_ff, D)),
            pl.BlockSpec((1, D), const),
        ],
        out_specs=pl.BlockSpec((tm, D), tok),
        out_shape=jax.ShapeDtypeStruct((T, D), F32),
        compiler_params=pltpu.CompilerParams(
            dimension_semantics=("arbitrary",),
            vmem_limit_bytes=VMEM_LIMIT_BYTES),
        name="out_mlp",
    )(x2, pool_n, y_attn, attn_out_g.reshape(1, d_attn), w_out, norm2_g.reshape(1, D), w_up, w_down,
      final_g.reshape(1, D))


def kernel(x, norm1_g, w_in, pool_w, pool_scale, pool_out_g, attn_out_g, w_out, norm2_g, w_up, w_down,
           final_g):
    B, S, D = x.shape
    pool_n, q, k, v = _in_proj_pool(x, norm1_g, w_in.astype(BF16), pool_w.astype(BF16), pool_scale,
                                    pool_out_g, tm=512)
    y_attn = _stickbreak(q, k, v, t=256)
    d_pool = pool_n.shape[-1]
    out = _out_mlp(x.reshape(B * S, D), pool_n.reshape(B * S, d_pool), y_attn.reshape(B * S, -1),
                   attn_out_g, w_out.astype(BF16), norm2_g, w_up.astype(BF16), w_down.astype(BF16),
                   final_g, tm=512, ff_chunk=1024)
    return out.reshape(B, S, D)
```

```python
import functools
import math

import jax
import jax.numpy as jnp
from jax import lax
from jax.experimental import pallas as pl
from jax.experimental.pallas import tpu as pltpu

EPS = 1e-6
LOG2_E = math.log2(math.e)
POOL_WINDOWS = (2, 4, 8, 16)
HEAD_DIM = 64
LANES = 128
EXIT_LOG2 = 152.0
POOL_HALO = 16
VMEM_LIMIT_BYTES = 56 * 1024 * 1024

BF16 = jnp.bfloat16
F32 = jnp.float32


def _rms(x, g):
    r = lax.rsqrt(jnp.mean(x * x, axis=-1, keepdims=True) + EPS)
    return x * r * g


def _in_proj_pool_kernel(x_ref, g1_ref, w_in_ref, pool_w_ref, pool_scale_ref, pool_g_ref,
                         pool_out_ref, q_ref, k_ref, v_ref, ubuf_ref, *, tm, d_pool, d_attn):
    s = pl.program_id(1)
    gdim = d_pool // len(POOL_WINDOWS)

    hn = _rms(x_ref[0], g1_ref[...]).astype(BF16)
    proj = jnp.dot(hn, w_in_ref[...], preferred_element_type=F32)

    u = proj[:, :d_pool]
    for p in range(d_attn // LANES):
        lo = d_pool + p * LANES
        q_ref[0, p] = (proj[:, lo:lo + LANES] * (LOG2_E * HEAD_DIM ** -0.5)).astype(BF16)
        k_ref[0, p] = proj[:, lo + d_attn:lo + d_attn + LANES].astype(BF16)
        v_ref[0, p] = proj[:, lo + 2 * d_attn:lo + 2 * d_attn + LANES].astype(BF16)

    @pl.when(s == 0)
    def _():
        ubuf_ref[0:POOL_HALO, :] = jnp.zeros((POOL_HALO, d_pool), F32)

    ubuf_ref[POOL_HALO:POOL_HALO + tm, :] = u

    pos = s * tm + lax.broadcasted_iota(jnp.int32, (tm, 1), 0)
    mapped = []
    for g, w in enumerate(POOL_WINDOWS):
        lo, hi = g * gdim, (g + 1) * gdim
        win = ubuf_ref[POOL_HALO:POOL_HALO + tm, lo:hi]
        for j in range(1, w):
            win = win + ubuf_ref[POOL_HALO - j:POOL_HALO - j + tm, lo:hi]
        count = jnp.minimum(pos + 1, w).astype(F32)
        pooled = win / count - u[:, lo:hi]
        mapped.append(jnp.dot(pooled.astype(BF16), pool_w_ref[g], preferred_element_type=F32))
    y = jnp.concatenate(mapped, axis=-1) * pool_scale_ref[...]
    pool_out_ref[0] = _rms(y, pool_g_ref[...]).astype(BF16)

    ubuf_ref[0:POOL_HALO, :] = ubuf_ref[tm:tm + POOL_HALO, :]


def _in_proj_pool(x, norm1_g, w_in, pool_w, pool_scale, pool_out_g, *, tm):
    B, S, D = x.shape
    n_groups, gdim, _ = pool_w.shape
    d_pool = n_groups * gdim
    d_attn = (w_in.shape[1] - d_pool) // 3
    kern = functools.partial(_in_proj_pool_kernel, tm=tm, d_pool=d_pool, d_attn=d_attn)
    const2 = lambda b, s: (0, 0)
    tok = lambda b, s: (b, s, 0)
    n_pairs = d_attn // LANES
    qkv_spec = pl.BlockSpec((1, n_pairs, tm, LANES), lambda b, s: (b, 0, s, 0))
    qkv_shape = jax.ShapeDtypeStruct((B, n_pairs, S, LANES), BF16)
    return pl.pallas_call(
        kern,
        grid=(B, S // tm),
        in_specs=[
            pl.BlockSpec((1, tm, D), tok),
            pl.BlockSpec((1, D), const2),
            pl.BlockSpec(w_in.shape, const2),
            pl.BlockSpec(pool_w.shape, lambda b, s: (0, 0, 0)),
            pl.BlockSpec((1, d_pool), const2),
            pl.BlockSpec((1, d_pool), const2),
        ],
        out_specs=[pl.BlockSpec((1, tm, d_pool), tok), qkv_spec, qkv_spec, qkv_spec],
        out_shape=[jax.ShapeDtypeStruct((B, S, d_pool), BF16), qkv_shape, qkv_shape, qkv_shape],
        scratch_shapes=[pltpu.VMEM((POOL_HALO + tm, d_pool), F32)],
        compiler_params=pltpu.CompilerParams(
            dimension_semantics=("arbitrary", "arbitrary"),
            vmem_limit_bytes=VMEM_LIMIT_BYTES),
        name="in_proj_pool",
    )(x, norm1_g.reshape(1, D), w_in, pool_w, pool_scale.reshape(1, d_pool),
      pool_out_g.reshape(1, d_pool))


def _stickbreak_kernel(q_ref, k_ref, v_ref, g_ref, o_ref, acc_ref, carry_ref, *, t):
    i = pl.program_id(1)
    n_pairs = q_ref.shape[1]
    n_heads = LANES // HEAD_DIM
    rows = n_heads * t
    lane = lax.broadcasted_iota(jnp.int32, (t, LANES), 1)
    key_r = lax.broadcasted_iota(jnp.int32, (t, t), 0)
    key_c = lax.broadcasted_iota(jnp.int32, (t, t), 1)
    suffix = (key_r > key_c).astype(BF16)
    qrow = lax.broadcasted_iota(jnp.int32, (rows, t), 0) & (t - 1)
    causal = lax.broadcasted_iota(jnp.int32, (rows, t), 1) < qrow

    def stacked_q(p):
        q2 = q_ref[0, p]
        return jnp.concatenate(
            [jnp.where((lane >= h * HEAD_DIM) & (lane < (h + 1) * HEAD_DIM), q2, jnp.zeros_like(q2))
             for h in range(n_heads)], axis=0)

    def blocks(p, qs, js, diag_first):
        sps, log_betas = [], []
        for n, j in enumerate(js):
            kb = k_ref[0, p, pl.ds(j * t, t), :]
            z = lax.dot_general(qs, kb, (((1,), (1,)), ((), ())), preferred_element_type=F32)
            sp = jnp.maximum(z, 0.0) + jnp.log2(1.0 + jnp.exp2(-jnp.abs(z)))
            log_betas.append(z - sp)
            if diag_first and n == 0:
                sp = jnp.where(causal, sp, 0.0)
            sps.append(sp)
        sp_all = jnp.concatenate(sps, axis=0)
        tail_all = jnp.dot(sp_all.astype(BF16), suffix, preferred_element_type=F32)
        out = []
        for n, j in enumerate(js):
            tail = tail_all[n * rows:(n + 1) * rows]
            a = jnp.exp2(log_betas[n] - tail)
            if diag_first and n == 0:
                a = jnp.where(causal, a, 0.0)
            vb = v_ref[0, p, pl.ds(j * t, t), :]
            pv = jnp.dot(a.astype(BF16), vb, preferred_element_type=F32)
            out.append((pv, tail[:, 0:1] + sps[n][:, 0:1]))
        return out

    def add_pair(p, qs, carry, acc, j0, j1, has_second, diag_first):
        (pv0, rs0), (pv1, rs1) = blocks(p, qs, [j0, j1], diag_first)
        acc = acc + jnp.exp2(-carry) * pv0
        carry = carry + rs0
        acc = acc + jnp.where(has_second, jnp.exp2(-carry), 0.0) * pv1
        carry = carry + jnp.where(has_second, rs1, 0.0)
        return carry, acc

    min_carry = None
    for p in range(n_pairs):
        carry, acc = add_pair(p, stacked_q(p), jnp.zeros((rows, 1), F32), jnp.zeros((rows, LANES), F32),
                              i, jnp.maximum(i - 1, 0), i >= 1, True)
        acc_ref[p] = acc
        carry_ref[p] = carry
        pair_min = jnp.min(carry)
        min_carry = pair_min if min_carry is None else jnp.minimum(min_carry, pair_min)

    @pl.when((i >= 2) & (min_carry < EXIT_LOG2))
    def _():
        def per_pair(p, _):
            qs = stacked_q(p)

            def more_blocks(state):
                n, pair_min, _, _ = state
                return (n < i // 2) & (pair_min < EXIT_LOG2)

            def pair_body(state):
                n, _, carry, acc = state
                j0 = i - 2 - 2 * n
                carry, acc = add_pair(p, qs, carry, acc, j0, jnp.maximum(j0 - 1, 0), j0 >= 1, False)
                return n + 1, jnp.min(carry), carry, acc

            carry = carry_ref[p]
            _, _, _, acc = lax.while_loop(more_blocks, pair_body, (0, jnp.min(carry), carry, acc_ref[p]))
            acc_ref[p] = acc
            return 0

        lax.fori_loop(0, n_pairs, per_pair, 0)

    y = jnp.concatenate([jnp.where(lane < HEAD_DIM, acc_ref[p, 0:t], acc_ref[p, t:2 * t])
                         for p in range(n_pairs)], axis=-1)
    o_ref[0] = _rms(y, g_ref[...]).astype(BF16)


def _stickbreak(q, k, v, attn_out_g, *, t):
    B, n_pairs, S, _ = q.shape
    d_attn = n_pairs * LANES
    rows = (LANES // HEAD_DIM) * t
    kern = functools.partial(_stickbreak_kernel, t=t)
    full = lambda b, i: (b, 0, 0, 0)
    return pl.pallas_call(
        kern,
        grid=(B, S // t),
        in_specs=[
            pl.BlockSpec((1, n_pairs, t, LANES), lambda b, i: (b, 0, i, 0)),
            pl.BlockSpec((1, n_pairs, S, LANES), full),
            pl.BlockSpec((1, n_pairs, S, LANES), full),
            pl.BlockSpec((1, d_attn), lambda b, i: (0, 0)),
        ],
        out_specs=pl.BlockSpec((1, t, d_attn), lambda b, i: (b, i, 0)),
        out_shape=jax.ShapeDtypeStruct((B, S, d_attn), BF16),
        scratch_shapes=[pltpu.VMEM((n_pairs, rows, LANES), F32), pltpu.VMEM((n_pairs, rows, 1), F32)],
        compiler_params=pltpu.CompilerParams(
            dimension_semantics=("arbitrary", "arbitrary"),
            vmem_limit_bytes=VMEM_LIMIT_BYTES),
        name="stickbreak",
    )(q, k, v, attn_out_g.reshape(1, d_attn))


def _out_mlp_kernel(x_ref, pool_ref, attn_ref, w_out_ref, g2_ref, w_up_ref, w_down_ref, gf_ref, o_ref, *,
                    ff_chunk):
    mixed = jnp.concatenate([pool_ref[...], attn_ref[...]], axis=-1)
    h = x_ref[...] + jnp.dot(mixed, w_out_ref[...], preferred_element_type=F32)

    hn2 = _rms(h, g2_ref[...]).astype(BF16)
    d_ff = w_up_ref.shape[1]
    acc = h
    for c in range(d_ff // ff_chunk):
        up = jnp.dot(hn2, w_up_ref[:, c * ff_chunk:(c + 1) * ff_chunk], preferred_element_type=F32)
        act = jnp.square(jnp.maximum(up, 0.0)).astype(BF16)
        acc = acc + jnp.dot(act, w_down_ref[c * ff_chunk:(c + 1) * ff_chunk, :],
                            preferred_element_type=F32)
    o_ref[...] = _rms(acc, gf_ref[...])


def _out_mlp(x2, pool_n, attn_n, w_out, norm2_g, w_up, w_down, final_g, *, tm, ff_chunk):
    T, D = x2.shape
    d_pool = pool_n.shape[1]
    d_attn = attn_n.shape[1]
    d_ff = w_up.shape[1]
    kern = functools.partial(_out_mlp_kernel, ff_chunk=ff_chunk)
    tok = lambda i: (i, 0)
    const = lambda i: (0, 0)
    resident = functools.partial(pl.BlockSpec, index_map=const, pipeline_mode=pl.Buffered(1))
    return pl.pallas_call(
        kern,
        grid=(T // tm,),
        in_specs=[
            pl.BlockSpec((tm, D), tok),
            pl.BlockSpec((tm, d_pool), tok),
            pl.BlockSpec((tm, d_attn), tok),
            resident((d_pool + d_attn, D)),
            pl.BlockSpec((1, D), const),
            resident((D, d_ff)),
            resident((d_ff, D)),
            pl.BlockSpec((1, D), const),
        ],
        out_specs=pl.BlockSpec((tm, D), tok),
        out_shape=jax.ShapeDtypeStruct((T, D), F32),
        compiler_params=pltpu.CompilerParams(
            dimension_semantics=("arbitrary",),
            vmem_limit_bytes=VMEM_LIMIT_BYTES),
        name="out_mlp",
    )(x2, pool_n, attn_n, w_out, norm2_g.reshape(1, D), w_up, w_down, final_g.reshape(1, D))


def kernel(x, norm1_g, w_in, pool_w, pool_scale, pool_out_g, attn_out_g, w_out, norm2_g, w_up, w_down,
           final_g):
    B, S, D = x.shape
    pool_n, q, k, v = _in_proj_pool(x, norm1_g, w_in.astype(BF16), pool_w.astype(BF16), pool_scale,
                                    pool_out_g, tm=512)
    attn_n = _stickbreak(q, k, v, attn_out_g, t=256)
    d_pool = pool_n.shape[-1]
    out = _out_mlp(x.reshape(B * S, D), pool_n.reshape(B * S, d_pool), attn_n.reshape(B * S, -1),
                   w_out.astype(BF16), norm2_g, w_up.astype(BF16), w_down.astype(BF16), final_g,
                   tm=512, ff_chunk=1024)
    return out.reshape(B, S, D)
```

```python
import functools
import math

import jax
import jax.numpy as jnp
from jax import lax
from jax.experimental import pallas as pl
from jax.experimental.pallas import tpu as pltpu

EPS = 1e-6
LOG2_E = math.log2(math.e)
POOL_WINDOWS = (2, 4, 8, 16)
HEAD_DIM = 64
LANES = 128
MASKED_SCORE = -1e30
EXIT_LOG2 = 152.0
POOL_HALO = 16
POOL_PAD = 8
VMEM_LIMIT_BYTES = 56 * 1024 * 1024

BF16 = jnp.bfloat16
F32 = jnp.float32


def _rms(x, g):
    r = lax.rsqrt(jnp.mean(x * x, axis=-1, keepdims=True) + EPS)
    return x * r * g


def _in_proj_pool_kernel(x_ref, g1_ref, w_in_ref, pool_w_ref, pool_scale_ref, pool_g_ref,
                         pool_out_ref, q_ref, k_ref, v_ref, ubuf_ref, s2_ref, s4_ref, s8_ref, *,
                         tm, d_pool, d_attn):
    s = pl.program_id(1)
    gdim = d_pool // len(POOL_WINDOWS)

    hn = _rms(x_ref[0], g1_ref[...]).astype(BF16)
    proj = jnp.dot(hn, w_in_ref[...], preferred_element_type=F32)

    u = proj[:, :d_pool]
    for p in range(d_attn // LANES):
        lo = d_pool + p * LANES
        q_ref[0, p] = (proj[:, lo:lo + LANES] * (LOG2_E * HEAD_DIM ** -0.5)).astype(BF16)
        k_ref[0, p] = proj[:, lo + d_attn:lo + d_attn + LANES].astype(BF16)
        v_ref[0, p] = proj[:, lo + 2 * d_attn:lo + 2 * d_attn + LANES].astype(BF16)

    off = POOL_PAD + POOL_HALO
    levels = (ubuf_ref, s2_ref, s4_ref, s8_ref)

    @pl.when(s == 0)
    def _():
        ubuf_ref[0:off, :] = jnp.zeros((off, d_pool), F32)
        for lvl in levels[1:]:
            lvl[0:POOL_PAD, :] = jnp.zeros((POOL_PAD, lvl.shape[1]), F32)

    ubuf_ref[off:off + tm, :] = u

    sums =[None] * len(POOL_WINDOWS)
    for n, w in enumerate(POOL_WINDOWS):
        shift = w // 2
        src = levels[n]
        if n + 1 < len(levels):
            doubled = src[POOL_PAD:off + tm, :] + src[POOL_PAD - shift:off + tm - shift, :]
            sums[n] = doubled[POOL_HALO:, 0:gdim]
            levels[n + 1][POOL_PAD:off + tm, :] = doubled[:, gdim:]
        else:
            sums[n] = src[off:off + tm, :] + src[off - shift:off + tm - shift, :]

    head_pos = s * tm + lax.broadcasted_iota(jnp.int32, (POOL_HALO, 1), 0)
    mapped = []
    for g, w in enumerate(POOL_WINDOWS):
        head_scale = w / jnp.minimum(head_pos + 1, w).astype(F32)
        mean = jnp.concatenate([sums[g][0:POOL_HALO] * head_scale, sums[g][POOL_HALO:]], axis=0) * (1.0 / w)
        pooled = mean - u[:, g * gdim:(g + 1) * gdim]
        mapped.append(jnp.dot(pooled.astype(BF16), pool_w_ref[g], preferred_element_type=F32))
    y = jnp.concatenate(mapped, axis=-1) * pool_scale_ref[...]
    pool_out_ref[0] = _rms(y, pool_g_ref[...]).astype(BF16)

    ubuf_ref[POOL_PAD:off, :] = ubuf_ref[tm + POOL_PAD:tm + off, :]


def _in_proj_pool(x, norm1_g, w_in, pool_w, pool_scale, pool_out_g, *, tm):
    B, S, D = x.shape
    n_groups, gdim, _ = pool_w.shape
    d_pool = n_groups * gdim
    d_attn = (w_in.shape[1] - d_pool) // 3
    kern = functools.partial(_in_proj_pool_kernel, tm=tm, d_pool=d_pool, d_attn=d_attn)
    const2 = lambda b, s: (0, 0)
    tok = lambda b, s: (b, s, 0)
    n_pairs = d_attn // LANES
    qkv_spec = pl.BlockSpec((1, n_pairs, tm, LANES), lambda b, s: (b, 0, s, 0))
    qkv_shape = jax.ShapeDtypeStruct((B, n_pairs, S, LANES), BF16)
    return pl.pallas_call(
        kern,
        grid=(B, S // tm),
        in_specs=[
            pl.BlockSpec((1, tm, D), tok),
            pl.BlockSpec((1, D), const2),
            pl.BlockSpec(w_in.shape, const2),
            pl.BlockSpec(pool_w.shape, lambda b, s: (0, 0, 0)),
            pl.BlockSpec((1, d_pool), const2),
            pl.BlockSpec((1, d_pool), const2),
        ],
        out_specs=[pl.BlockSpec((1, tm, d_pool), tok), qkv_spec, qkv_spec, qkv_spec],
        out_shape=[jax.ShapeDtypeStruct((B, S, d_pool), BF16), qkv_shape, qkv_shape, qkv_shape],
        scratch_shapes=[pltpu.VMEM((POOL_PAD + POOL_HALO + tm, d_pool - n * gdim), F32)
                        for n in range(n_groups)],
        compiler_params=pltpu.CompilerParams(
            dimension_semantics=("arbitrary", "arbitrary"),
            vmem_limit_bytes=VMEM_LIMIT_BYTES),
        name="in_proj_pool",
    )(x, norm1_g.reshape(1, D), w_in, pool_w, pool_scale.reshape(1, d_pool),
      pool_out_g.reshape(1, d_pool))


def _stickbreak_kernel(q_ref, k_ref, v_ref, g_ref, o_ref, acc_ref, carry_ref, *, t):
    i = pl.program_id(1)
    n_pairs = q_ref.shape[1]
    n_heads = LANES // HEAD_DIM
    rows = n_heads * t
    lane = lax.broadcasted_iota(jnp.int32, (t, LANES), 1)
    key_r = lax.broadcasted_iota(jnp.int32, (t, t), 0)
    key_c = lax.broadcasted_iota(jnp.int32, (t, t), 1)
    suffix = (key_r > key_c).astype(BF16)
    qrow = lax.broadcasted_iota(jnp.int32, (rows, t), 0) & (t - 1)
    causal = lax.broadcasted_iota(jnp.int32, (rows, t), 1) < qrow

    def stacked_q(p):
        q2 = q_ref[0, p]
        return jnp.concatenate(
            [jnp.where((lane >= h * HEAD_DIM) & (lane < (h + 1) * HEAD_DIM), q2, jnp.zeros_like(q2))
             for h in range(n_heads)], axis=0)

    def blocks(p, qs, js, diag_first):
        sps, log_betas = [], []
        for n, j in enumerate(js):
            kb = k_ref[0, p, pl.ds(j * t, t), :]
            z = lax.dot_general(qs, kb, (((1,), (1,)), ((), ())), preferred_element_type=F32)
            if diag_first and n == 0:
                z = jnp.where(causal, z, MASKED_SCORE)
            sp = jnp.maximum(z, 0.0) + jnp.log2(1.0 + jnp.exp2(-jnp.abs(z)))
            log_betas.append(z - sp)
            sps.append(sp)
        sp_all = jnp.concatenate(sps, axis=0)
        tail_all = jnp.dot(sp_all.astype(BF16), suffix, preferred_element_type=F32)
        out = []
        for n, j in enumerate(js):
            tail = tail_all[n * rows:(n + 1) * rows]
            a = jnp.exp2(log_betas[n] - tail)
            vb = v_ref[0, p, pl.ds(j * t, t), :]
            pv = jnp.dot(a.astype(BF16), vb, preferred_element_type=F32)
            out.append((pv, tail[:, 0:1] + sps[n][:, 0:1]))
        return out

    def add_pair(p, qs, carry, acc, j0, j1, has_second, diag_first):
        (pv0, rs0), (pv1, rs1) = blocks(p, qs, [j0, j1], diag_first)
        acc = acc + jnp.exp2(-carry) * pv0
        carry = carry + rs0
        acc = acc + jnp.where(has_second, jnp.exp2(-carry), 0.0) * pv1
        carry = carry + jnp.where(has_second, rs1, 0.0)
        return carry, acc

    min_carry = None
    for p in range(n_pairs):
        carry, acc = add_pair(p, stacked_q(p), jnp.zeros((rows, 1), F32), jnp.zeros((rows, LANES), F32),
                              i, jnp.maximum(i - 1, 0), i >= 1, True)
        acc_ref[p] = acc
        carry_ref[p] = carry
        pair_min = jnp.min(carry)
        min_carry = pair_min if min_carry is None else jnp.minimum(min_carry, pair_min)

    @pl.when((i >= 2) & (min_carry < EXIT_LOG2))
    def _():
        def per_pair(p, _):
            qs = stacked_q(p)

            def more_blocks(state):
                n, pair_min, _, _ = state
                return (n < i // 2) & (pair_min < EXIT_LOG2)

            def pair_body(state):
                n, _, carry, acc = state
                j0 = i - 2 - 2 * n
                carry, acc = add_pair(p, qs, carry, acc, j0, jnp.maximum(j0 - 1, 0), j0 >= 1, False)
                return n + 1, jnp.min(carry), carry, acc

            carry = carry_ref[p]
            _, _, _, acc = lax.while_loop(more_blocks, pair_body, (0, jnp.min(carry), carry, acc_ref[p]))
            acc_ref[p] = acc
            return 0

        lax.fori_loop(0, n_pairs, per_pair, 0)

    y = jnp.concatenate([jnp.where(lane < HEAD_DIM, acc_ref[p, 0:t], acc_ref[p, t:2 * t])
                         for p in range(n_pairs)], axis=-1)
    o_ref[0] = _rms(y, g_ref[...]).astype(BF16)


def _stickbreak(q, k, v, attn_out_g, *, t):
    B, n_pairs, S, _ = q.shape
    d_attn = n_pairs * LANES
    rows = (LANES // HEAD_DIM) * t
    kern = functools.partial(_stickbreak_kernel, t=t)
    full = lambda b, i: (b, 0, 0, 0)
    return pl.pallas_call(
        kern,
        grid=(B, S // t),
        in_specs=[
            pl.BlockSpec((1, n_pairs, t, LANES), lambda b, i: (b, 0, i, 0)),
            pl.BlockSpec((1, n_pairs, S, LANES), full),
            pl.BlockSpec((1, n_pairs, S, LANES), full),
            pl.BlockSpec((1, d_attn), lambda b, i: (0, 0)),
        ],
        out_specs=pl.BlockSpec((1, t, d_attn), lambda b, i: (b, i, 0)),
        out_shape=jax.ShapeDtypeStruct((B, S, d_attn), BF16),
        scratch_shapes=[pltpu.VMEM((n_pairs, rows, LANES), F32), pltpu.VMEM((n_pairs, rows, 1), F32)],
        compiler_params=pltpu.CompilerParams(
            dimension_semantics=("arbitrary", "arbitrary"),
            vmem_limit_bytes=VMEM_LIMIT_BYTES),
        name="stickbreak",
    )(q, k, v, attn_out_g.reshape(1, d_attn))


def _out_mlp_kernel(x_ref, pool_ref, attn_ref, w_out_ref, g2_ref, w_up_ref, w_down_ref, gf_ref, o_ref, *,
                    ff_chunk):
    mixed = jnp.concatenate([pool_ref[...], attn_ref[...]], axis=-1)
    h = x_ref[...] + jnp.dot(mixed, w_out_ref[...], preferred_element_type=F32)

    hn2 = _rms(h, g2_ref[...]).astype(BF16)
    d_ff = w_up_ref.shape[1]
    acc = h
    for c in range(d_ff // ff_chunk):
        up = jnp.dot(hn2, w_up_ref[:, c * ff_chunk:(c + 1) * ff_chunk], preferred_element_type=F32)
        act = jnp.square(jnp.maximum(up, 0.0)).astype(BF16)
        acc = acc + jnp.dot(act, w_down_ref[c * ff_chunk:(c + 1) * ff_chunk, :],
                            preferred_element_type=F32)
    o_ref[...] = _rms(acc, gf_ref[...])


def _out_mlp(x2, pool_n, attn_n, w_out, norm2_g, w_up, w_down, final_g, *, tm, ff_chunk):
    T, D = x2.shape
    d_pool = pool_n.shape[1]
    d_attn = attn_n.shape[1]
    d_ff = w_up.shape[1]
    kern = functools.partial(_out_mlp_kernel, ff_chunk=ff_chunk)
    tok = lambda i: (i, 0)
    const = lambda i: (0, 0)
    resident = functools.partial(pl.BlockSpec, index_map=const, pipeline_mode=pl.Buffered(1))
    return pl.pallas_call(
        kern,
        grid=(T // tm,),
        in_specs=[
            pl.BlockSpec((tm, D), tok),
            pl.BlockSpec((tm, d_pool), tok),
            pl.BlockSpec((tm, d_attn), tok),
            resident((d_pool + d_attn, D)),
            pl.BlockSpec((1, D), const),
            resident((D, d_ff)),
            resident((d_ff, D)),
            pl.BlockSpec((1, D), const),
        ],
        out_specs=pl.BlockSpec((tm, D), tok),
        out_shape=jax.ShapeDtypeStruct((T, D), F32),
        compiler_params=pltpu.CompilerParams(
            dimension_semantics=("arbitrary",),
            vmem_limit_bytes=VMEM_LIMIT_BYTES),
        name="out_mlp",
    )(x2, pool_n, attn_n, w_out, norm2_g.reshape(1, D), w_up, w_down, final_g.reshape(1, D))


def kernel(x, norm1_g, w_in, pool_w, pool_scale, pool_out_g, attn_out_g, w_out, norm2_g, w_up, w_down,
           final_g):
    B, S, D = x.shape
    pool_n, q, k, v = _in_proj_pool(x, norm1_g, w_in.astype(BF16), pool_w.astype(BF16), pool_scale,
                                    pool_out_g, tm=512)
    attn_n = _stickbreak(q, k, v, attn_out_g, t=256)
    d_pool = pool_n.shape[-1]
    out = _out_mlp(x.reshape(B * S, D), pool_n.reshape(B * S, d_pool), attn_n.reshape(B * S, -1),
                   w_out.astype(BF16), norm2_g, w_up.astype(BF16), w_down.astype(BF16), final_g,
                   tm=512, ff_chunk=1024)
    return out.reshape(B, S, D)
```

```python
import functools
import math

import jax
import jax.numpy as jnp
from jax import lax
from jax.experimental import pallas as pl
from jax.experimental.pallas import tpu as pltpu

EPS = 1e-6
LOG2_E = math.log2(math.e)
POOL_WINDOWS = (2, 4, 8, 16)
HEAD_DIM = 64
LANES = 128
MASKED_SCORE = -1e30
EXIT_LOG2 = 152.0
POOL_HALO = 16
POOL_PAD = 8
VMEM_LIMIT_BYTES = 56 * 1024 * 1024

BF16 = jnp.bfloat16
F32 = jnp.float32


def _rms(x, g):
    r = lax.rsqrt(jnp.mean(x * x, axis=-1, keepdims=True) + EPS)
    return x * r * g


def _in_proj_pool_kernel(x_ref, g1_ref, w_in_ref, pool_w_ref, pool_scale_ref, pool_g_ref,
                         pool_out_ref, q_ref, k_ref, v_ref, w_bf_ref, ubuf_ref, s2_ref, s4_ref, s8_ref, *,
                         tm, tiles_per_seq, n_tiles, d_pool, d_attn):
    g = pl.program_id(0)
    gdim = d_pool // len(POOL_WINDOWS)
    off = POOL_PAD + POOL_HALO
    levels = (ubuf_ref, s2_ref, s4_ref, s8_ref)

    @pl.when(g == 0)
    def _():
        w_bf_ref[...] = w_in_ref[...].astype(BF16)
        ubuf_ref[...] = jnp.zeros_like(ubuf_ref)
        for lvl in levels[1:]:
            lvl[0:POOL_PAD, :] = jnp.zeros((POOL_PAD, lvl.shape[1]), F32)

    hn = _rms(x_ref[...], g1_ref[...]).astype(BF16)
    n_pairs = d_attn // LANES

    def project(first_col, width):
        return jnp.dot(hn, w_bf_ref[:, first_col:first_col + width], preferred_element_type=F32)

    def emit_heads(out_ref, first_col, scale=None):
        cols = project(first_col, d_attn)
        for p in range(n_pairs):
            blk = cols[:, p * LANES:(p + 1) * LANES]
            out_ref[0, p] = (blk if scale is None else blk * scale).astype(BF16)

    sums = [None] * len(POOL_WINDOWS)

    def window_level(n):
        shift = POOL_WINDOWS[n] // 2
        src = levels[n]
        if n + 1 < len(levels):
            doubled = src[POOL_PAD:off + tm, :] + src[POOL_PAD - shift:off + tm - shift, :]
            sums[n] = doubled[POOL_HALO:, 0:gdim]
            levels[n + 1][POOL_PAD:off + tm, :] = doubled[:, gdim:]
        else:
            sums[n] = src[off:off + tm, :] + src[off - shift:off + tm - shift, :]

    emit_heads(q_ref, d_pool, LOG2_E * HEAD_DIM ** -0.5)
    window_level(0)
    window_level(1)
    emit_heads(k_ref, d_pool + d_attn)
    window_level(2)
    window_level(3)
    emit_heads(v_ref, d_pool + 2 * d_attn)

    seq_tile = jnp.maximum(g - 1, 0) % tiles_per_seq
    head_pos = seq_tile * tm + lax.broadcasted_iota(jnp.int32, (POOL_HALO, 1), 0)
    mapped = []
    for grp, w in enumerate(POOL_WINDOWS):
        head_scale = w / jnp.minimum(head_pos + 1, w).astype(F32)
        mean = jnp.concatenate([sums[grp][0:POOL_HALO] * head_scale, sums[grp][POOL_HALO:]], axis=0)
        mean = mean * (1.0 / w)
        pooled = mean - ubuf_ref[off:off + tm, grp * gdim:(grp + 1) * gdim]
        mapped.append(jnp.dot(pooled.astype(BF16), pool_w_ref[grp].astype(BF16),
                              preferred_element_type=F32))
    y = jnp.concatenate(mapped, axis=-1) * pool_scale_ref[...]
    pool_out_ref[...] = _rms(y, pool_g_ref[...]).astype(BF16)

    starts_seq = jnp.minimum(g, n_tiles - 1) % tiles_per_seq == 0
    ubuf_ref[POOL_PAD:off, :] = jnp.where(starts_seq, 0.0, ubuf_ref[tm + POOL_PAD:tm + off, :])

    ubuf_ref[off:off + tm, :] = project(0, d_pool)


def _in_proj_pool(x2, seq_len, norm1_g, w_in, pool_w, pool_scale, pool_out_g, *, tm):
    T, D = x2.shape
    n_groups, gdim, _ = pool_w.shape
    d_pool = n_groups * gdim
    d_attn = (w_in.shape[1] - d_pool) // 3
    tiles_per_seq = seq_len // tm
    n_tiles = T // tm
    n_pairs = d_attn // LANES
    kern = functools.partial(_in_proj_pool_kernel, tm=tm, tiles_per_seq=tiles_per_seq, n_tiles=n_tiles,
                             d_pool=d_pool, d_attn=d_attn)

    def proj_tile(g):
        return jnp.minimum(g, n_tiles - 1)

    const2 = lambda g: (0, 0)
    qkv_spec = pl.BlockSpec((1, n_pairs, tm, LANES),
                            lambda g: (proj_tile(g) // tiles_per_seq, 0, proj_tile(g) % tiles_per_seq, 0))
    qkv_shape = jax.ShapeDtypeStruct((T // seq_len, n_pairs, seq_len, LANES), BF16)
    return pl.pallas_call(
        kern,
        grid=(n_tiles + 1,),
        in_specs=[
            pl.BlockSpec((tm, D), lambda g: (proj_tile(g), 0)),
            pl.BlockSpec((1, D), const2),
            pl.BlockSpec(w_in.shape, const2, pipeline_mode=pl.Buffered(1)),
            pl.BlockSpec(pool_w.shape, lambda g: (0, 0, 0)),
            pl.BlockSpec((1, d_pool), const2),
            pl.BlockSpec((1, d_pool), const2),
        ],
        out_specs=[pl.BlockSpec((tm, d_pool), lambda g: (jnp.maximum(g - 1, 0), 0)),
                   qkv_spec, qkv_spec, qkv_spec],
        out_shape=[jax.ShapeDtypeStruct((T, d_pool), BF16), qkv_shape, qkv_shape, qkv_shape],
        scratch_shapes=[pltpu.VMEM(w_in.shape, BF16)] +
                       [pltpu.VMEM((POOL_PAD + POOL_HALO + tm, d_pool - n * gdim), F32)
                        for n in range(n_groups)],
        compiler_params=pltpu.CompilerParams(
            dimension_semantics=("arbitrary",),
            vmem_limit_bytes=VMEM_LIMIT_BYTES),
        name="in_proj_pool",
    )(x2, norm1_g.reshape(1, D), w_in, pool_w, pool_scale.reshape(1, d_pool),
      pool_out_g.reshape(1, d_pool))


def _stickbreak_kernel(q_ref, k_ref, v_ref, g_ref, o_ref, acc_ref, carry_ref, *, t):
    i = pl.program_id(1)
    n_pairs = q_ref.shape[1]
    n_heads = LANES // HEAD_DIM
    rows = n_heads * t
    lane = lax.broadcasted_iota(jnp.int32, (t, LANES), 1)
    key_r = lax.broadcasted_iota(jnp.int32, (t, t), 0)
    key_c = lax.broadcasted_iota(jnp.int32, (t, t), 1)
    suffix = (key_r > key_c).astype(BF16)
    qrow = lax.broadcasted_iota(jnp.int32, (rows, t), 0) & (t - 1)
    causal = lax.broadcasted_iota(jnp.int32, (rows, t), 1) < qrow

    def stacked_q(p):
        q2 = q_ref[0, p]
        return jnp.concatenate(
            [jnp.where((lane >= h * HEAD_DIM) & (lane < (h + 1) * HEAD_DIM), q2, jnp.zeros_like(q2))
             for h in range(n_heads)], axis=0)

    def blocks(p, qs, js, diag_first):
        sps, log_betas = [], []
        for n, j in enumerate(js):
            kb = k_ref[0, p, pl.ds(j * t, t), :]
            z = lax.dot_general(qs, kb, (((1,), (1,)), ((), ())), preferred_element_type=F32)
            if diag_first and n == 0:
                z = jnp.where(causal, z, MASKED_SCORE)
            sp = jnp.maximum(z, 0.0) + jnp.log2(1.0 + jnp.exp2(-jnp.abs(z)))
            log_betas.append(z - sp)
            sps.append(sp)
        sp_all = jnp.concatenate(sps, axis=0)
        tail_all = jnp.dot(sp_all.astype(BF16), suffix, preferred_element_type=F32)
        out = []
        for n, j in enumerate(js):
            tail = tail_all[n * rows:(n + 1) * rows]
            a = jnp.exp2(log_betas[n] - tail)
            vb = v_ref[0, p, pl.ds(j * t, t), :]
            pv = jnp.dot(a.astype(BF16), vb, preferred_element_type=F32)
            out.append((pv, tail[:, 0:1] + sps[n][:, 0:1]))
        return out

    def add_pair(p, qs, carry, acc, j0, j1, has_second, diag_first):
        (pv0, rs0), (pv1, rs1) = blocks(p, qs, [j0, j1], diag_first)
        acc = acc + jnp.exp2(-carry) * pv0
        carry = carry + rs0
        acc = acc + jnp.where(has_second, jnp.exp2(-carry), 0.0) * pv1
        carry = carry + jnp.where(has_second, rs1, 0.0)
        return carry, acc

    min_carry = None
    for p in range(n_pairs):
        carry, acc = add_pair(p, stacked_q(p), jnp.zeros((rows, 1), F32), jnp.zeros((rows, LANES), F32),
                              i, jnp.maximum(i - 1, 0), i >= 1, True)
        acc_ref[p] = acc
        carry_ref[p] = carry
        pair_min = jnp.min(carry)
        min_carry = pair_min if min_carry is None else jnp.minimum(min_carry, pair_min)

    @pl.when((i >= 2) & (min_carry < EXIT_LOG2))
    def _():
        def per_pair(p, _):
            qs = stacked_q(p)

            def more_blocks(state):
                n, pair_min, _, _ = state
                return (n < i // 2) & (pair_min < EXIT_LOG2)

            def pair_body(state):
                n, _, carry, acc = state
                j0 = i - 2 - 2 * n
                carry, acc = add_pair(p, qs, carry, acc, j0, jnp.maximum(j0 - 1, 0), j0 >= 1, False)
                return n + 1, jnp.min(carry), carry, acc

            carry = carry_ref[p]
            _, _, _, acc = lax.while_loop(more_blocks, pair_body, (0, jnp.min(carry), carry, acc_ref[p]))
            acc_ref[p] = acc
            return 0

        lax.fori_loop(0, n_pairs, per_pair, 0)

    y = jnp.concatenate([jnp.where(lane < HEAD_DIM, acc_ref[p, 0:t], acc_ref[p, t:2 * t])
                         for p in range(n_pairs)], axis=-1)
    o_ref[0] = _rms(y, g_ref[...]).astype(BF16)


def _stickbreak(q, k, v, attn_out_g, *, t):
    B, n_pairs, S, _ = q.shape
    d_attn = n_pairs * LANES
    rows = (LANES // HEAD_DIM) * t
    kern = functools.partial(_stickbreak_kernel, t=t)
    full = lambda b, i: (b, 0, 0, 0)
    return pl.pallas_call(
        kern,
        grid=(B, S // t),
        in_specs=[
            pl.BlockSpec((1, n_pairs, t, LANES), lambda b, i: (b, 0, i, 0)),
            pl.BlockSpec((1, n_pairs, S, LANES), full),
            pl.BlockSpec((1, n_pairs, S, LANES), full),
            pl.BlockSpec((1, d_attn), lambda b, i: (0, 0)),
        ],
        out_specs=pl.BlockSpec((1, t, d_attn), lambda b, i: (b, i, 0)),
        out_shape=jax.ShapeDtypeStruct((B, S, d_attn), BF16),
        scratch_shapes=[pltpu.VMEM((n_pairs, rows, LANES), F32), pltpu.VMEM((n_pairs, rows, 1), F32)],
        compiler_params=pltpu.CompilerParams(
            dimension_semantics=("arbitrary", "arbitrary"),
            vmem_limit_bytes=VMEM_LIMIT_BYTES),
        name="stickbreak",
    )(q, k, v, attn_out_g.reshape(1, d_attn))


def _out_mlp_kernel(x_ref, pool_ref, attn_ref, w_out_ref, g2_ref, w_up_ref, w_down_ref, gf_ref, o_ref, *,
                    ff_chunk):
    mixed = jnp.concatenate([pool_ref[...], attn_ref[...]], axis=-1)
    h = x_ref[...] + jnp.dot(mixed, w_out_ref[...], preferred_element_type=F32)

    hn2 = _rms(h, g2_ref[...]).astype(BF16)
    d_ff = w_up_ref.shape[1]
    acc = h
    for c in range(d_ff // ff_chunk):
        up = jnp.dot(hn2, w_up_ref[:, c * ff_chunk:(c + 1) * ff_chunk], preferred_element_type=F32)
        act = jnp.square(jnp.maximum(up, 0.0)).astype(BF16)
        acc = acc + jnp.dot(act, w_down_ref[c * ff_chunk:(c + 1) * ff_chunk, :],
                            preferred_element_type=F32)
    o_ref[...] = _rms(acc, gf_ref[...])


def _out_mlp(x2, pool_n, attn_n, w_out, norm2_g, w_up, w_down, final_g, *, tm, ff_chunk):
    T, D = x2.shape
    d_pool = pool_n.shape[1]
    d_attn = attn_n.shape[1]
    d_ff = w_up.shape[1]
    kern = functools.partial(_out_mlp_kernel, ff_chunk=ff_chunk)
    tok = lambda i: (i, 0)
    const = lambda i: (0, 0)
    resident = functools.partial(pl.BlockSpec, index_map=const, pipeline_mode=pl.Buffered(1))
    return pl.pallas_call(
        kern,
        grid=(T // tm,),
        in_specs=[
            pl.BlockSpec((tm, D), tok),
            pl.BlockSpec((tm, d_pool), tok),
            pl.BlockSpec((tm, d_attn), tok),
            resident((d_pool + d_attn, D)),
            pl.BlockSpec((1, D), const),
            resident((D, d_ff)),
            resident((d_ff, D)),
            pl.BlockSpec((1, D), const),
        ],
        out_specs=pl.BlockSpec((tm, D), tok),
        out_shape=jax.ShapeDtypeStruct((T, D), F32),
        compiler_params=pltpu.CompilerParams(
            dimension_semantics=("arbitrary",),
            vmem_limit_bytes=VMEM_LIMIT_BYTES),
        name="out_mlp",
    )(x2, pool_n, attn_n, w_out, norm2_g.reshape(1, D), w_up, w_down, final_g.reshape(1, D))


def kernel(x, norm1_g, w_in, pool_w, pool_scale, pool_out_g, attn_out_g, w_out, norm2_g, w_up, w_down,
           final_g):
    B, S, D = x.shape
    x2 = x.reshape(B * S, D)
    pool_n, q, k, v = _in_proj_pool(x2, S, norm1_g, w_in, pool_w, pool_scale, pool_out_g, tm=512)
    attn_n = _stickbreak(q, k, v, attn_out_g, t=256)
    out = _out_mlp(x2, pool_n, attn_n.reshape(B * S, -1), w_out.astype(BF16), norm2_g, w_up.astype(BF16),
                   w_down.astype(BF16), final_g, tm=1024, ff_chunk=512)
    return out.reshape(B, S, D)
```

```python
import functools
import math

import jax
import jax.numpy as jnp
from jax import lax
from jax.experimental import pallas as pl
from jax.experimental.pallas import tpu as pltpu

EPS = 1e-6
LOG2_E = math.log2(math.e)
POOL_WINDOWS = (2, 4, 8, 16)
HEAD_DIM = 64
LANES = 128
MASKED_SCORE = -1e30
EXIT_LOG2 = 152.0
POOL_HALO = 16
POOL_PAD = 8
VMEM_LIMIT_BYTES = 56 * 1024 * 1024

BF16 = jnp.bfloat16
F32 = jnp.float32


def _rms(x, g):
    r = lax.rsqrt(jnp.mean(x * x, axis=-1, keepdims=True) + EPS)
    return x * r * g


def _in_proj_pool_kernel(x_ref, g1_ref, w_in_ref, pool_w_ref, pool_scale_ref, pool_g_ref,
                         pool_out_ref, q_ref, k_ref, v_ref, w_bf_ref, ubuf_ref, s2_ref, s4_ref, s8_ref, *,
                         tm, tiles_per_seq, n_tiles, d_pool, d_attn):
    g = pl.program_id(0)
    gdim = d_pool // len(POOL_WINDOWS)
    off = POOL_PAD + POOL_HALO
    levels = (ubuf_ref, s2_ref, s4_ref, s8_ref)

    @pl.when(g == 0)
    def _():
        w_bf_ref[...] = w_in_ref[...].astype(BF16)
        ubuf_ref[...] = jnp.zeros_like(ubuf_ref)
        for lvl in levels[1:]:
            lvl[0:POOL_PAD, :] = jnp.zeros((POOL_PAD, lvl.shape[1]), F32)

    hn = _rms(x_ref[...], g1_ref[...]).astype(BF16)
    n_pairs = d_attn // LANES

    def project(first_col, width):
        return jnp.dot(hn, w_bf_ref[:, first_col:first_col + width], preferred_element_type=F32)

    def emit_heads(out_ref, first_col, scale=None):
        cols = project(first_col, d_attn)
        for p in range(n_pairs):
            blk = cols[:, p * LANES:(p + 1) * LANES]
            out_ref[0, p] = (blk if scale is None else blk * scale).astype(BF16)

    sums = [None] * len(POOL_WINDOWS)

    def window_level(n):
        shift = POOL_WINDOWS[n] // 2
        src = levels[n]
        if n + 1 < len(levels):
            doubled = src[POOL_PAD:off + tm, :] + src[POOL_PAD - shift:off + tm - shift, :]
            sums[n] = doubled[POOL_HALO:, 0:gdim]
            levels[n + 1][POOL_PAD:off + tm, :] = doubled[:, gdim:]
        else:
            sums[n] = src[off:off + tm, :] + src[off - shift:off + tm - shift, :]

    emit_heads(q_ref, d_pool, LOG2_E * HEAD_DIM ** -0.5)
    window_level(0)
    window_level(1)
    emit_heads(k_ref, d_pool + d_attn)
    window_level(2)
    window_level(3)
    emit_heads(v_ref, d_pool + 2 * d_attn)

    seq_tile = jnp.maximum(g - 1, 0) % tiles_per_seq
    head_pos = seq_tile * tm + lax.broadcasted_iota(jnp.int32, (POOL_HALO, 1), 0)
    mapped = []
    for grp, w in enumerate(POOL_WINDOWS):
        head_scale = w / jnp.minimum(head_pos + 1, w).astype(F32)
        mean = jnp.concatenate([sums[grp][0:POOL_HALO] * head_scale, sums[grp][POOL_HALO:]], axis=0)
        mean = mean * (1.0 / w)
        pooled = mean - ubuf_ref[off:off + tm, grp * gdim:(grp + 1) * gdim]
        mapped.append(jnp.dot(pooled.astype(BF16), pool_w_ref[grp].astype(BF16),
                              preferred_element_type=F32))
    y = jnp.concatenate(mapped, axis=-1) * pool_scale_ref[...]
    pool_out_ref[...] = _rms(y, pool_g_ref[...]).astype(BF16)

    starts_seq = jnp.minimum(g, n_tiles - 1) % tiles_per_seq == 0
    ubuf_ref[POOL_PAD:off, :] = jnp.where(starts_seq, 0.0, ubuf_ref[tm + POOL_PAD:tm + off, :])

    ubuf_ref[off:off + tm, :] = project(0, d_pool)


def _in_proj_pool(x2, seq_len, norm1_g, w_in, pool_w, pool_scale, pool_out_g, *, tm):
    T, D = x2.shape
    n_groups, gdim, _ = pool_w.shape
    d_pool = n_groups * gdim
    d_attn = (w_in.shape[1] - d_pool) // 3
    tiles_per_seq = seq_len // tm
    n_tiles = T // tm
    n_pairs = d_attn // LANES
    kern = functools.partial(_in_proj_pool_kernel, tm=tm, tiles_per_seq=tiles_per_seq, n_tiles=n_tiles,
                             d_pool=d_pool, d_attn=d_attn)

    def proj_tile(g):
        return jnp.minimum(g, n_tiles - 1)

    const2 = lambda g: (0, 0)
    qkv_spec = pl.BlockSpec((1, n_pairs, tm, LANES),
                            lambda g: (proj_tile(g) // tiles_per_seq, 0, proj_tile(g) % tiles_per_seq, 0))
    qkv_shape = jax.ShapeDtypeStruct((T // seq_len, n_pairs, seq_len, LANES), BF16)
    return pl.pallas_call(
        kern,
        grid=(n_tiles + 1,),
        in_specs=[
            pl.BlockSpec((tm, D), lambda g: (proj_tile(g), 0)),
            pl.BlockSpec((1, D), const2),
            pl.BlockSpec(w_in.shape, const2, pipeline_mode=pl.Buffered(1)),
            pl.BlockSpec(pool_w.shape, lambda g: (0, 0, 0)),
            pl.BlockSpec((1, d_pool), const2),
            pl.BlockSpec((1, d_pool), const2),
        ],
        out_specs=[pl.BlockSpec((tm, d_pool), lambda g: (jnp.maximum(g - 1, 0), 0)),
                   qkv_spec, qkv_spec, qkv_spec],
        out_shape=[jax.ShapeDtypeStruct((T, d_pool), BF16), qkv_shape, qkv_shape, qkv_shape],
        scratch_shapes=[pltpu.VMEM(w_in.shape, BF16)] +
                       [pltpu.VMEM((POOL_PAD + POOL_HALO + tm, d_pool - n * gdim), F32)
                        for n in range(n_groups)],
        compiler_params=pltpu.CompilerParams(
            dimension_semantics=("arbitrary",),
            vmem_limit_bytes=VMEM_LIMIT_BYTES),
        name="in_proj_pool",
    )(x2, norm1_g.reshape(1, D), w_in, pool_w, pool_scale.reshape(1, d_pool),
      pool_out_g.reshape(1, d_pool))


def _stickbreak_kernel(q_ref, k_ref, v_ref, g_ref, o_ref, acc_ref, carry_ref, *, t, blocks_per_step):
    n_pairs = q_ref.shape[1]
    n_heads = LANES // HEAD_DIM
    rows = n_heads * t
    lane = lax.broadcasted_iota(jnp.int32, (t, LANES), 1)
    key_r = lax.broadcasted_iota(jnp.int32, (t, t), 0)
    key_c = lax.broadcasted_iota(jnp.int32, (t, t), 1)
    suffix = (key_r > key_c).astype(BF16)
    qrow = lax.broadcasted_iota(jnp.int32, (rows, t), 0) & (t - 1)
    causal = lax.broadcasted_iota(jnp.int32, (rows, t), 1) < qrow

    def query_block(r, _):
        i = pl.program_id(1) * blocks_per_step + r
        q_rows = pl.ds(pl.multiple_of(r * t, t), t)

        def stacked_q(p):
            q2 = q_ref[0, p, q_rows, :]
            return jnp.concatenate(
                [jnp.where((lane >= h * HEAD_DIM) & (lane < (h + 1) * HEAD_DIM), q2, jnp.zeros_like(q2))
                 for h in range(n_heads)], axis=0)

        def blocks(p, qs, js, diag_first):
            sps, log_betas = [], []
            for n, j in enumerate(js):
                kb = k_ref[0, p, pl.ds(j * t, t), :]
                z = lax.dot_general(qs, kb, (((1,), (1,)), ((), ())), preferred_element_type=F32)
                if diag_first and n == 0:
                    z = jnp.where(causal, z, MASKED_SCORE)
                sp = jnp.maximum(z, 0.0) + jnp.log2(1.0 + jnp.exp2(-jnp.abs(z)))
                log_betas.append(z - sp)
                sps.append(sp)
            sp_all = jnp.concatenate(sps, axis=0)
            tail_all = jnp.dot(sp_all.astype(BF16), suffix, preferred_element_type=F32)
            out = []
            for n, j in enumerate(js):
                tail = tail_all[n * rows:(n + 1) * rows]
                a = jnp.exp2(log_betas[n] - tail)
                vb = v_ref[0, p, pl.ds(j * t, t), :]
                pv = jnp.dot(a.astype(BF16), vb, preferred_element_type=F32)
                out.append((pv, tail[:, 0:1] + sps[n][:, 0:1]))
            return out

        def add_pair(p, qs, carry, acc, j0, j1, has_second, diag_first):
            (pv0, rs0), (pv1, rs1) = blocks(p, qs, [j0, j1], diag_first)
            acc = acc + jnp.exp2(-carry) * pv0
            carry = carry + rs0
            acc = acc + jnp.where(has_second, jnp.exp2(-carry), 0.0) * pv1
            carry = carry + jnp.where(has_second, rs1, 0.0)
            return carry, acc

        min_carry = None
        for p in range(n_pairs):
            carry, acc = add_pair(p, stacked_q(p), jnp.zeros((rows, 1), F32), jnp.zeros((rows, LANES), F32),
                                  i, jnp.maximum(i - 1, 0), i >= 1, True)
            acc_ref[p] = acc
            carry_ref[p] = carry
            pair_min = jnp.min(carry)
            min_carry = pair_min if min_carry is None else jnp.minimum(min_carry, pair_min)

        @pl.when((i >= 2) & (min_carry < EXIT_LOG2))
        def _():
            def per_pair(p, _):
                qs = stacked_q(p)

                def more_blocks(state):
                    n, pair_min, _, _ = state
                    return (n < i // 2) & (pair_min < EXIT_LOG2)

                def pair_body(state):
                    n, _, carry, acc = state
                    j0 = i - 2 - 2 * n
                    carry, acc = add_pair(p, qs, carry, acc, j0, jnp.maximum(j0 - 1, 0), j0 >= 1, False)
                    return n + 1, jnp.min(carry), carry, acc

                carry = carry_ref[p]
                _, _, _, acc = lax.while_loop(more_blocks, pair_body,
                                              (0, jnp.min(carry), carry, acc_ref[p]))
                acc_ref[p] = acc
                return 0

            lax.fori_loop(0, n_pairs, per_pair, 0)

        y = jnp.concatenate([jnp.where(lane < HEAD_DIM, acc_ref[p, 0:t], acc_ref[p, t:2 * t])
                             for p in range(n_pairs)], axis=-1)
        o_ref[0, q_rows, :] = _rms(y, g_ref[...]).astype(BF16)
        return 0

    lax.fori_loop(0, blocks_per_step, query_block, 0)


def _stickbreak(q, k, v, attn_out_g, *, t, blocks_per_step):
    B, n_pairs, S, _ = q.shape
    d_attn = n_pairs * LANES
    rows = (LANES // HEAD_DIM) * t
    ts = t * blocks_per_step
    kern = functools.partial(_stickbreak_kernel, t=t, blocks_per_step=blocks_per_step)
    full = lambda b, i: (b, 0, 0, 0)
    return pl.pallas_call(
        kern,
        grid=(B, S // ts),
        in_specs=[
            pl.BlockSpec((1, n_pairs, ts, LANES), lambda b, i: (b, 0, i, 0)),
            pl.BlockSpec((1, n_pairs, S, LANES), full),
            pl.BlockSpec((1, n_pairs, S, LANES), full),
            pl.BlockSpec((1, d_attn), lambda b, i: (0, 0)),
        ],
        out_specs=pl.BlockSpec((1, ts, d_attn), lambda b, i: (b, i, 0)),
        out_shape=jax.ShapeDtypeStruct((B, S, d_attn), BF16),
        scratch_shapes=[pltpu.VMEM((n_pairs, rows, LANES), F32), pltpu.VMEM((n_pairs, rows, 1), F32)],
        compiler_params=pltpu.CompilerParams(
            dimension_semantics=("arbitrary", "arbitrary"),
            vmem_limit_bytes=VMEM_LIMIT_BYTES),
        name="stickbreak",
    )(q, k, v, attn_out_g.reshape(1, d_attn))


def _out_mlp_kernel(x_ref, pool_ref, attn_ref, w_out_ref, g2_ref, w_up_ref, w_down_ref, gf_ref, o_ref, *,
                    ff_chunk):
    mixed = jnp.concatenate([pool_ref[...], attn_ref[...]], axis=-1)
    h = x_ref[...] + jnp.dot(mixed, w_out_ref[...], preferred_element_type=F32)

    hn2 = _rms(h, g2_ref[...]).astype(BF16)
    d_ff = w_up_ref.shape[1]
    acc = h
    for c in range(d_ff // ff_chunk):
        up = jnp.dot(hn2, w_up_ref[:, c * ff_chunk:(c + 1) * ff_chunk], preferred_element_type=F32)
        act = jnp.square(jnp.maximum(up, 0.0)).astype(BF16)
        acc = acc + jnp.dot(act, w_down_ref[c * ff_chunk:(c + 1) * ff_chunk, :],
                            preferred_element_type=F32)
    o_ref[...] = _rms(acc, gf_ref[...])


def _out_mlp(x2, pool_n, attn_n, w_out, norm2_g, w_up, w_down, final_g, *, tm, ff_chunk):
    T, D = x2.shape
    d_pool = pool_n.shape[1]
    d_attn = attn_n.shape[1]
    d_ff = w_up.shape[1]
    kern = functools.partial(_out_mlp_kernel, ff_chunk=ff_chunk)
    tok = lambda i: (i, 0)
    const = lambda i: (0, 0)
    resident = functools.partial(pl.BlockSpec, index_map=const, pipeline_mode=pl.Buffered(1))
    return pl.pallas_call(
        kern,
        grid=(T // tm,),
        in_specs=[
            pl.BlockSpec((tm, D), tok),
            pl.BlockSpec((tm, d_pool), tok),
            pl.BlockSpec((tm, d_attn), tok),
            resident((d_pool + d_attn, D)),
            pl.BlockSpec((1, D), const),
            resident((D, d_ff)),
            resident((d_ff, D)),
            pl.BlockSpec((1, D), const),
        ],
        out_specs=pl.BlockSpec((tm, D), tok),
        out_shape=jax.ShapeDtypeStruct((T, D), F32),
        compiler_params=pltpu.CompilerParams(
            dimension_semantics=("arbitrary",),
            vmem_limit_bytes=VMEM_LIMIT_BYTES),
        name="out_mlp",
    )(x2, pool_n, attn_n, w_out, norm2_g.reshape(1, D), w_up, w_down, final_g.reshape(1, D))


def kernel(x, norm1_g, w_in, pool_w, pool_scale, pool_out_g, attn_out_g, w_out, norm2_g, w_up, w_down,
           final_g):
    B, S, D = x.shape
    x2 = x.reshape(B * S, D)
    pool_n, q, k, v = _in_proj_pool(x2, S, norm1_g, w_in, pool_w, pool_scale, pool_out_g, tm=512)
    attn_n = _stickbreak(q, k, v, attn_out_g, t=256, blocks_per_step=4)
    out = _out_mlp(x2, pool_n, attn_n.reshape(B * S, -1), w_out.astype(BF16), norm2_g, w_up.astype(BF16),
                   w_down.astype(BF16), final_g, tm=1024, ff_chunk=512)
    return out.reshape(B, S, D)
```

```python
import functools
import math

import jax
import jax.numpy as jnp
from jax import lax
from jax.experimental import pallas as pl
from jax.experimental.pallas import tpu as pltpu

EPS = 1e-6
LOG2_E = math.log2(math.e)
POOL_WINDOWS = (2, 4, 8, 16)
HEAD_DIM = 64
LANES = 128
MASKED_SCORE = -1e30
EXIT_LOG2 = 152.0
POOL_HALO = 16
POOL_PAD = 8
VMEM_LIMIT_BYTES = 56 * 1024 * 1024

BF16 = jnp.bfloat16
F32 = jnp.float32


def _rms(x, g):
    r = lax.rsqrt(jnp.mean(x * x, axis=-1, keepdims=True) + EPS)
    return x * r * g


def _in_proj_pool_kernel(x_ref, g1_ref, w_in_ref, pool_w_ref, pool_scale_ref, pool_g_ref, later_w_refs,
                         pool_out_ref, q_ref, k_ref, v_ref, later_w_bf_refs,
                         w_bf_ref, ubuf_ref, s2_ref, s4_ref, s8_ref, *,
                         tm, tiles_per_seq, n_tiles, d_pool, d_attn):
    g = pl.program_id(0)
    gdim = d_pool // len(POOL_WINDOWS)
    off = POOL_PAD + POOL_HALO
    levels = (ubuf_ref, s2_ref, s4_ref, s8_ref)

    @pl.when(g == 0)
    def _():
        w_bf_ref[...] = w_in_ref[...].astype(BF16)
        ubuf_ref[...] = jnp.zeros_like(ubuf_ref)
        for lvl in levels[1:]:
            lvl[0:POOL_PAD, :] = jnp.zeros((POOL_PAD, lvl.shape[1]), F32)

    for w_ref, w_bf in zip(later_w_refs, later_w_bf_refs):
        w_bf[...] = w_ref[...].astype(BF16)

    hn = _rms(x_ref[...], g1_ref[...]).astype(BF16)
    n_pairs = d_attn // LANES

    def project(first_col, width):
        return jnp.dot(hn, w_bf_ref[:, first_col:first_col + width], preferred_element_type=F32)

    def emit_heads(out_ref, first_col, scale=None):
        cols = project(first_col, d_attn)
        for p in range(n_pairs):
            blk = cols[:, p * LANES:(p + 1) * LANES]
            out_ref[0, p] = (blk if scale is None else blk * scale).astype(BF16)

    sums = [None] * len(POOL_WINDOWS)

    def window_level(n):
        shift = POOL_WINDOWS[n] // 2
        src = levels[n]
        if n + 1 < len(levels):
            doubled = src[POOL_PAD:off + tm, :] + src[POOL_PAD - shift:off + tm - shift, :]
            sums[n] = doubled[POOL_HALO:, 0:gdim]
            levels[n + 1][POOL_PAD:off + tm, :] = doubled[:, gdim:]
        else:
            sums[n] = src[off:off + tm, :] + src[off - shift:off + tm - shift, :]

    emit_heads(q_ref, d_pool, LOG2_E * HEAD_DIM ** -0.5)
    window_level(0)
    window_level(1)
    emit_heads(k_ref, d_pool + d_attn)
    window_level(2)
    window_level(3)
    emit_heads(v_ref, d_pool + 2 * d_attn)

    seq_tile = jnp.maximum(g - 1, 0) % tiles_per_seq
    head_pos = seq_tile * tm + lax.broadcasted_iota(jnp.int32, (POOL_HALO, 1), 0)
    mapped = []
    for grp, w in enumerate(POOL_WINDOWS):
        head_scale = w / jnp.minimum(head_pos + 1, w).astype(F32)
        mean = jnp.concatenate([sums[grp][0:POOL_HALO] * head_scale, sums[grp][POOL_HALO:]], axis=0)
        mean = mean * (1.0 / w)
        pooled = mean - ubuf_ref[off:off + tm, grp * gdim:(grp + 1) * gdim]
        mapped.append(jnp.dot(pooled.astype(BF16), pool_w_ref[grp].astype(BF16),
                              preferred_element_type=F32))
    y = jnp.concatenate(mapped, axis=-1) * pool_scale_ref[...]
    pool_out_ref[...] = _rms(y, pool_g_ref[...]).astype(BF16)

    starts_seq = jnp.minimum(g, n_tiles - 1) % tiles_per_seq == 0
    ubuf_ref[POOL_PAD:off, :] = jnp.where(starts_seq, 0.0, ubuf_ref[tm + POOL_PAD:tm + off, :])

    ubuf_ref[off:off + tm, :] = project(0, d_pool)


def _in_proj_pool(x2, seq_len, norm1_g, w_in, pool_w, pool_scale, pool_out_g, later_weights, *, tm):
    T, D = x2.shape
    n_groups, gdim, _ = pool_w.shape
    d_pool = n_groups * gdim
    d_attn = (w_in.shape[1] - d_pool) // 3
    tiles_per_seq = seq_len // tm
    n_tiles = T // tm
    n_pairs = d_attn // LANES
    kern = functools.partial(_in_proj_pool_kernel, tm=tm, tiles_per_seq=tiles_per_seq, n_tiles=n_tiles,
                             d_pool=d_pool, d_attn=d_attn)

    def proj_tile(g):
        return jnp.minimum(g, n_tiles - 1)

    def slab_spec(w, axis):
        shape = list(w.shape)
        shape[axis] //= n_tiles
        return pl.BlockSpec(tuple(shape), lambda g: tuple(proj_tile(g) if a == axis else 0 for a in range(2)))

    for w, axis in later_weights:
        assert w.ndim == 2 and w.shape[axis] % (n_tiles * 16) == 0, (w.shape, axis, n_tiles)
    slab_specs = [slab_spec(w, axis) for w, axis in later_weights]

    const2 = lambda g: (0, 0)
    qkv_spec = pl.BlockSpec((1, n_pairs, tm, LANES),
                            lambda g: (proj_tile(g) // tiles_per_seq, 0, proj_tile(g) % tiles_per_seq, 0))
    qkv_shape = jax.ShapeDtypeStruct((T // seq_len, n_pairs, seq_len, LANES), BF16)
    return pl.pallas_call(
        kern,
        grid=(n_tiles + 1,),
        in_specs=[
            pl.BlockSpec((tm, D), lambda g: (proj_tile(g), 0)),
            pl.BlockSpec((1, D), const2),
            pl.BlockSpec(w_in.shape, const2, pipeline_mode=pl.Buffered(1)),
            pl.BlockSpec(pool_w.shape, lambda g: (0, 0, 0)),
            pl.BlockSpec((1, d_pool), const2),
            pl.BlockSpec((1, d_pool), const2),
            slab_specs,
        ],
        out_specs=[pl.BlockSpec((tm, d_pool), lambda g: (jnp.maximum(g - 1, 0), 0)),
                   qkv_spec, qkv_spec, qkv_spec, slab_specs],
        out_shape=[jax.ShapeDtypeStruct((T, d_pool), BF16), qkv_shape, qkv_shape, qkv_shape,
                   [jax.ShapeDtypeStruct(w.shape, BF16) for w, _ in later_weights]],
        scratch_shapes=[pltpu.VMEM(w_in.shape, BF16)] +
                       [pltpu.VMEM((POOL_PAD + POOL_HALO + tm, d_pool - n * gdim), F32)
                        for n in range(n_groups)],
        compiler_params=pltpu.CompilerParams(
            dimension_semantics=("arbitrary",),
            vmem_limit_bytes=VMEM_LIMIT_BYTES),
        name="in_proj_pool",
    )(x2, norm1_g.reshape(1, D), w_in, pool_w, pool_scale.reshape(1, d_pool),
      pool_out_g.reshape(1, d_pool), [w for w, _ in later_weights])


def _stickbreak_kernel(q_ref, k_ref, v_ref, g_ref, o_ref, acc_ref, carry_ref, *, t, blocks_per_step):
    n_pairs = q_ref.shape[1]
    n_heads = LANES // HEAD_DIM
    rows = n_heads * t
    lane = lax.broadcasted_iota(jnp.int32, (t, LANES), 1)
    key_r = lax.broadcasted_iota(jnp.int32, (t, t), 0)
    key_c = lax.broadcasted_iota(jnp.int32, (t, t), 1)
    later_or_same = (key_r >= key_c).astype(BF16)
    qrow = lax.broadcasted_iota(jnp.int32, (rows, t), 0) & (t - 1)
    causal = lax.broadcasted_iota(jnp.int32, (rows, t), 1) < qrow

    def query_block(r, _):
        i = pl.program_id(1) * blocks_per_step + r
        q_rows = pl.ds(pl.multiple_of(r * t, t), t)

        def stacked_q(p):
            q2 = q_ref[0, p, q_rows, :]
            return jnp.concatenate(
                [jnp.where((lane >= h * HEAD_DIM) & (lane < (h + 1) * HEAD_DIM), q2, jnp.zeros_like(q2))
                 for h in range(n_heads)], axis=0)

        def pair_stages(p, qs, state, j0, j1, has_second, result):
            js = (j0, j1)
            zs, sps = [], []
            for n, j in enumerate(js):
                kb = k_ref[0, p, pl.ds(j * t, t), :]
                z = lax.dot_general(qs, kb, (((1,), (1,)), ((), ())), preferred_element_type=F32)
                if state is None and n == 0:
                    z = jnp.where(causal, z, MASKED_SCORE)
                zs.append(z)
                sps.append(jnp.maximum(z, 0.0) + jnp.log2(1.0 + jnp.exp2(-jnp.abs(z))))
            yield
            sp_all = jnp.concatenate(sps, axis=0)
            from_here_all = jnp.dot(sp_all.astype(BF16), later_or_same, preferred_element_type=F32)
            yield
            pvs, block_sums = [], []
            for n, j in enumerate(js):
                from_here = from_here_all[n * rows:(n + 1) * rows]
                a = jnp.exp2(zs[n] - from_here)
                vb = v_ref[0, p, pl.ds(j * t, t), :]
                pvs.append(jnp.dot(a.astype(BF16), vb, preferred_element_type=F32))
                block_sums.append(from_here[:, 0:1])
            if state is None:
                carry, acc = block_sums[0], pvs[0]
            else:
                carry, acc = state
                acc = acc + jnp.exp2(-carry) * pvs[0]
                carry = carry + block_sums[0]
            acc = acc + jnp.where(has_second, jnp.exp2(-carry), 0.0) * pvs[1]
            carry = carry + jnp.where(has_second, block_sums[1], 0.0)
            result.append((carry, acc))

        results = [[] for _ in range(n_pairs)]
        stages = [pair_stages(p, stacked_q(p), None, i, jnp.maximum(i - 1, 0), i >= 1, results[p])
                  for p in range(n_pairs)]
        n_stages = 3
        for tick in range(n_pairs + n_stages - 1):
            for p in range(n_pairs):
                if 0 <= tick - p < n_stages:
                    next(stages[p], None)

        min_carry = None
        for p in range(n_pairs):
            carry, acc = results[p][0]
            acc_ref[p] = acc
            carry_ref[p] = carry
            pair_min = jnp.min(carry)
            min_carry = pair_min if min_carry is None else jnp.minimum(min_carry, pair_min)

        @pl.when((i >= 2) & (min_carry < EXIT_LOG2))
        def _():
            def per_pair(p, _):
                qs = stacked_q(p)

                def more_blocks(state):
                    n, pair_min, _, _ = state
                    return (n < i // 2) & (pair_min < EXIT_LOG2)

                def pair_body(state):
                    n, _, carry, acc = state
                    j0 = i - 2 - 2 * n
                    result = []
                    for _ in pair_stages(p, qs, (carry, acc), j0, jnp.maximum(j0 - 1, 0), j0 >= 1, result):
                        pass
                    carry, acc = result[0]
                    return n + 1, jnp.min(carry), carry, acc

                carry = carry_ref[p]
                _, _, _, acc = lax.while_loop(more_blocks, pair_body,
                                              (0, jnp.min(carry), carry, acc_ref[p]))
                acc_ref[p] = acc
                return 0

            lax.fori_loop(0, n_pairs, per_pair, 0)

        y = jnp.concatenate([jnp.where(lane < HEAD_DIM, acc_ref[p, 0:t], acc_ref[p, t:2 * t])
                             for p in range(n_pairs)], axis=-1)
        o_ref[0, q_rows, :] = _rms(y, g_ref[...]).astype(BF16)
        return 0

    lax.fori_loop(0, blocks_per_step, query_block, 0)


def _stickbreak(q, k, v, attn_out_g, *, t, blocks_per_step):
    B, n_pairs, S, _ = q.shape
    d_attn = n_pairs * LANES
    rows = (LANES // HEAD_DIM) * t
    ts = t * blocks_per_step
    kern = functools.partial(_stickbreak_kernel, t=t, blocks_per_step=blocks_per_step)
    full = lambda b, i: (b, 0, 0, 0)
    return pl.pallas_call(
        kern,
        grid=(B, S // ts),
        in_specs=[
            pl.BlockSpec((1, n_pairs, ts, LANES), lambda b, i: (b, 0, i, 0)),
            pl.BlockSpec((1, n_pairs, S, LANES), full),
            pl.BlockSpec((1, n_pairs, S, LANES), full),
            pl.BlockSpec((1, d_attn), lambda b, i: (0, 0)),
        ],
        out_specs=pl.BlockSpec((1, ts, d_attn), lambda b, i: (b, i, 0)),
        out_shape=jax.ShapeDtypeStruct((B, S, d_attn), BF16),
        scratch_shapes=[pltpu.VMEM((n_pairs, rows, LANES), F32), pltpu.VMEM((n_pairs, rows, 1), F32)],
        compiler_params=pltpu.CompilerParams(
            dimension_semantics=("arbitrary", "arbitrary"),
            vmem_limit_bytes=VMEM_LIMIT_BYTES),
        name="stickbreak",
    )(q, k, v, attn_out_g.reshape(1, d_attn))


def _out_mlp_kernel(x_ref, pool_ref, attn_ref, w_out_ref, g2_ref, w_up_ref, w_down_ref, gf_ref, o_ref, *,
                    ff_chunk):
    mixed = jnp.concatenate([pool_ref[...], attn_ref[...]], axis=-1)
    h = x_ref[...] + jnp.dot(mixed, w_out_ref[...], preferred_element_type=F32)

    hn2 = _rms(h, g2_ref[...]).astype(BF16)
    d_ff = w_up_ref.shape[1]
    acc = h
    for c in range(d_ff // ff_chunk):
        up = jnp.dot(hn2, w_up_ref[:, c * ff_chunk:(c + 1) * ff_chunk], preferred_element_type=F32)
        act = jnp.square(jnp.maximum(up, 0.0)).astype(BF16)
        acc = acc + jnp.dot(act, w_down_ref[c * ff_chunk:(c + 1) * ff_chunk, :],
                            preferred_element_type=F32)
    o_ref[...] = _rms(acc, gf_ref[...])


def _out_mlp(x2, pool_n, attn_n, w_out, norm2_g, w_up, w_down, final_g, *, tm, ff_chunk):
    T, D = x2.shape
    d_pool = pool_n.shape[1]
    d_attn = attn_n.shape[1]
    d_ff = w_up.shape[1]
    kern = functools.partial(_out_mlp_kernel, ff_chunk=ff_chunk)
    tok = lambda i: (i, 0)
    const = lambda i: (0, 0)
    resident = functools.partial(pl.BlockSpec, index_map=const, pipeline_mode=pl.Buffered(1))
    return pl.pallas_call(
        kern,
        grid=(T // tm,),
        in_specs=[
            pl.BlockSpec((tm, D), tok),
            pl.BlockSpec((tm, d_pool), tok),
            pl.BlockSpec((tm, d_attn), tok),
            resident((d_pool + d_attn, D)),
            pl.BlockSpec((1, D), const),
            resident((D, d_ff)),
            resident((d_ff, D)),
            pl.BlockSpec((1, D), const),
        ],
        out_specs=pl.BlockSpec((tm, D), tok),
        out_shape=jax.ShapeDtypeStruct((T, D), F32),
        compiler_params=pltpu.CompilerParams(
            dimension_semantics=("arbitrary",),
            vmem_limit_bytes=VMEM_LIMIT_BYTES),
        name="out_mlp",
    )(x2, pool_n, attn_n, w_out, norm2_g.reshape(1, D), w_up, w_down, final_g.reshape(1, D))


def kernel(x, norm1_g, w_in, pool_w, pool_scale, pool_out_g, attn_out_g, w_out, norm2_g, w_up, w_down,
           final_g):
    B, S, D = x.shape
    x2 = x.reshape(B * S, D)
    pool_n, q, k, v, (w_out_bf, w_up_bf, w_down_bf) = _in_proj_pool(
        x2, S, norm1_g, w_in, pool_w, pool_scale, pool_out_g, [(w_out, 0), (w_up, 1), (w_down, 0)], tm=512)
    attn_n = _stickbreak(q, k, v, attn_out_g, t=256, blocks_per_step=4)
    out = _out_mlp(x2, pool_n, attn_n.reshape(B * S, -1), w_out_bf, norm2_g, w_up_bf, w_down_bf, final_g,
                   tm=1024, ff_chunk=512)
    return out.reshape(B, S, D)
```

```python
import functools
import math

import jax
import jax.numpy as jnp
from jax import lax
from jax.experimental import pallas as pl
from jax.experimental.pallas import tpu as pltpu

EPS = 1e-6
LOG2_E = math.log2(math.e)
POOL_WINDOWS = (2, 4, 8, 16)
HEAD_DIM = 64
LANES = 128
MASKED_SCORE = -1e30
EXIT_LOG2 = 152.0
BLOCKS_PER_GROUP = 4
POOL_HALO = 16
POOL_PAD = 8
VMEM_LIMIT_BYTES = 56 * 1024 * 1024

BF16 = jnp.bfloat16
F32 = jnp.float32


def _rms(x, g):
    r = lax.rsqrt(jnp.mean(x * x, axis=-1, keepdims=True) + EPS)
    return x * r * g


def _in_proj_pool_kernel(x_ref, g1_ref, w_in_ref, pool_w_ref, pool_scale_ref, pool_g_ref, later_w_refs,
                         pool_out_ref, q_ref, k_ref, v_ref, later_w_bf_refs,
                         w_bf_ref, ubuf_ref, s2_ref, s4_ref, s8_ref, *,
                         tm, tiles_per_seq, n_tiles, d_pool, d_attn):
    g = pl.program_id(0)
    gdim = d_pool // len(POOL_WINDOWS)
    off = POOL_PAD + POOL_HALO
    levels = (ubuf_ref, s2_ref, s4_ref, s8_ref)

    @pl.when(g == 0)
    def _():
        w_bf_ref[...] = w_in_ref[...].astype(BF16)
        ubuf_ref[...] = jnp.zeros_like(ubuf_ref)
        for lvl in levels[1:]:
            lvl[0:POOL_PAD, :] = jnp.zeros((POOL_PAD, lvl.shape[1]), F32)

    for w_ref, w_bf in zip(later_w_refs, later_w_bf_refs):
        w_bf[...] = w_ref[...].astype(BF16)

    hn = _rms(x_ref[...], g1_ref[...]).astype(BF16)
    n_pairs = d_attn // LANES

    def project(first_col, width):
        return jnp.dot(hn, w_bf_ref[:, first_col:first_col + width], preferred_element_type=F32)

    def emit_heads(out_ref, first_col, scale=None):
        cols = project(first_col, d_attn)
        for p in range(n_pairs):
            blk = cols[:, p * LANES:(p + 1) * LANES]
            out_ref[0, p] = (blk if scale is None else blk * scale).astype(BF16)

    sums = [None] * len(POOL_WINDOWS)

    def window_level(n):
        shift = POOL_WINDOWS[n] // 2
        src = levels[n]
        if n + 1 < len(levels):
            doubled = src[POOL_PAD:off + tm, :] + src[POOL_PAD - shift:off + tm - shift, :]
            sums[n] = doubled[POOL_HALO:, 0:gdim]
            levels[n + 1][POOL_PAD:off + tm, :] = doubled[:, gdim:]
        else:
            sums[n] = src[off:off + tm, :] + src[off - shift:off + tm - shift, :]

    emit_heads(q_ref, d_pool, LOG2_E * HEAD_DIM ** -0.5)
    window_level(0)
    window_level(1)
    emit_heads(k_ref, d_pool + d_attn)
    window_level(2)
    window_level(3)
    emit_heads(v_ref, d_pool + 2 * d_attn)

    seq_tile = jnp.maximum(g - 1, 0) % tiles_per_seq
    head_pos = seq_tile * tm + lax.broadcasted_iota(jnp.int32, (POOL_HALO, 1), 0)
    mapped = []
    for grp, w in enumerate(POOL_WINDOWS):
        head_scale = w / jnp.minimum(head_pos + 1, w).astype(F32)
        mean = jnp.concatenate([sums[grp][0:POOL_HALO] * head_scale, sums[grp][POOL_HALO:]], axis=0)
        mean = mean * (1.0 / w)
        pooled = mean - ubuf_ref[off:off + tm, grp * gdim:(grp + 1) * gdim]
        mapped.append(jnp.dot(pooled.astype(BF16), pool_w_ref[grp].astype(BF16),
                              preferred_element_type=F32))
    y = jnp.concatenate(mapped, axis=-1) * pool_scale_ref[...]
    pool_out_ref[...] = _rms(y, pool_g_ref[...]).astype(BF16)

    starts_seq = jnp.minimum(g, n_tiles - 1) % tiles_per_seq == 0
    ubuf_ref[POOL_PAD:off, :] = jnp.where(starts_seq, 0.0, ubuf_ref[tm + POOL_PAD:tm + off, :])

    ubuf_ref[off:off + tm, :] = project(0, d_pool)


def _in_proj_pool(x2, seq_len, norm1_g, w_in, pool_w, pool_scale, pool_out_g, later_weights, *, tm):
    T, D = x2.shape
    n_groups, gdim, _ = pool_w.shape
    d_pool = n_groups * gdim
    d_attn = (w_in.shape[1] - d_pool) // 3
    tiles_per_seq = seq_len // tm
    n_tiles = T // tm
    n_pairs = d_attn // LANES
    kern = functools.partial(_in_proj_pool_kernel, tm=tm, tiles_per_seq=tiles_per_seq, n_tiles=n_tiles,
                             d_pool=d_pool, d_attn=d_attn)

    def proj_tile(g):
        return jnp.minimum(g, n_tiles - 1)

    def slab_spec(w, axis):
        shape = list(w.shape)
        shape[axis] //= n_tiles
        return pl.BlockSpec(tuple(shape), lambda g: tuple(proj_tile(g) if a == axis else 0 for a in range(2)))

    for w, axis in later_weights:
        assert w.ndim == 2 and w.shape[axis] % (n_tiles * 16) == 0, (w.shape, axis, n_tiles)
    slab_specs = [slab_spec(w, axis) for w, axis in later_weights]

    const2 = lambda g: (0, 0)
    qkv_spec = pl.BlockSpec((1, n_pairs, tm, LANES),
                            lambda g: (proj_tile(g) // tiles_per_seq, 0, proj_tile(g) % tiles_per_seq, 0))
    qkv_shape = jax.ShapeDtypeStruct((T // seq_len, n_pairs, seq_len, LANES), BF16)
    return pl.pallas_call(
        kern,
        grid=(n_tiles + 1,),
        in_specs=[
            pl.BlockSpec((tm, D), lambda g: (proj_tile(g), 0)),
            pl.BlockSpec((1, D), const2),
            pl.BlockSpec(w_in.shape, const2, pipeline_mode=pl.Buffered(1)),
            pl.BlockSpec(pool_w.shape, lambda g: (0, 0, 0)),
            pl.BlockSpec((1, d_pool), const2),
            pl.BlockSpec((1, d_pool), const2),
            slab_specs,
        ],
        out_specs=[pl.BlockSpec((tm, d_pool), lambda g: (jnp.maximum(g - 1, 0), 0)),
                   qkv_spec, qkv_spec, qkv_spec, slab_specs],
        out_shape=[jax.ShapeDtypeStruct((T, d_pool), BF16), qkv_shape, qkv_shape, qkv_shape,
                   [jax.ShapeDtypeStruct(w.shape, BF16) for w, _ in later_weights]],
        scratch_shapes=[pltpu.VMEM(w_in.shape, BF16)] +
                       [pltpu.VMEM((POOL_PAD + POOL_HALO + tm, d_pool - n * gdim), F32)
                        for n in range(n_groups)],
        compiler_params=pltpu.CompilerParams(
            dimension_semantics=("arbitrary",),
            vmem_limit_bytes=VMEM_LIMIT_BYTES),
        name="in_proj_pool",
    )(x2, norm1_g.reshape(1, D), w_in, pool_w, pool_scale.reshape(1, d_pool),
      pool_out_g.reshape(1, d_pool), [w for w, _ in later_weights])


def _stickbreak_kernel(q_ref, k_ref, v_ref, g_ref, o_ref, acc_ref, carry_ref, *, t, blocks_per_step):
    n_pairs = q_ref.shape[1]
    n_heads = LANES // HEAD_DIM
    rows = n_heads * t
    lane = lax.broadcasted_iota(jnp.int32, (t, LANES), 1)
    key_r = lax.broadcasted_iota(jnp.int32, (t, t), 0)
    key_c = lax.broadcasted_iota(jnp.int32, (t, t), 1)
    later_or_same = (key_r >= key_c).astype(BF16)
    qrow = lax.broadcasted_iota(jnp.int32, (rows, t), 0) & (t - 1)
    causal = lax.broadcasted_iota(jnp.int32, (rows, t), 1) < qrow

    def query_group(grp, _):
        first = grp * BLOCKS_PER_GROUP
        blk_i = [pl.program_id(1) * blocks_per_step + first + u for u in range(BLOCKS_PER_GROUP)]

        def q_rows(u):
            return pl.ds(pl.multiple_of((first + u) * t, t), t)

        def stacked_q(u, p):
            q2 = q_ref[0, p, q_rows(u), :]
            return jnp.concatenate(
                [jnp.where((lane >= h * HEAD_DIM) & (lane < (h + 1) * HEAD_DIM), q2, jnp.zeros_like(q2))
                 for h in range(n_heads)], axis=0)

        def pair_stages(p, qs, state, j0, j1, has_second, result):
            js = (j0, j1)
            zs, sps = [], []
            for n, j in enumerate(js):
                kb = k_ref[0, p, pl.ds(j * t, t), :]
                z = lax.dot_general(qs, kb, (((1,), (1,)), ((), ())), preferred_element_type=F32)
                if state is None and n == 0:
                    z = jnp.where(causal, z, MASKED_SCORE)
                zs.append(z)
                sps.append(jnp.maximum(z, 0.0) + jnp.log2(1.0 + jnp.exp2(-jnp.abs(z))))
            yield
            sp_all = jnp.concatenate(sps, axis=0)
            from_here_all = jnp.dot(sp_all.astype(BF16), later_or_same, preferred_element_type=F32)
            yield
            pvs, block_sums = [], []
            for n, j in enumerate(js):
                from_here = from_here_all[n * rows:(n + 1) * rows]
                a = jnp.exp2(zs[n] - from_here)
                vb = v_ref[0, p, pl.ds(j * t, t), :]
                pvs.append(jnp.dot(a.astype(BF16), vb, preferred_element_type=F32))
                block_sums.append(from_here[:, 0:1])
            if state is None:
                carry, acc = block_sums[0], pvs[0]
            else:
                carry, acc = state
                acc = acc + jnp.exp2(-carry) * pvs[0]
                carry = carry + block_sums[0]
            acc = acc + jnp.where(has_second, jnp.exp2(-carry), 0.0) * pvs[1]
            carry = carry + jnp.where(has_second, block_sums[1], 0.0)
            result.append((carry, acc))

        chains = [(u, p) for u in range(BLOCKS_PER_GROUP) for p in range(n_pairs)]
        results = [[] for _ in chains]
        stages = [pair_stages(p, stacked_q(u, p), None, blk_i[u], jnp.maximum(blk_i[u] - 1, 0),
                              blk_i[u] >= 1, results[c]) for c, (u, p) in enumerate(chains)]
        n_stages = 3
        for tick in range(len(chains) + n_stages - 1):
            for c in range(len(chains)):
                if 0 <= tick - c < n_stages:
                    next(stages[c], None)

        needs_more = None
        for c, (u, p) in enumerate(chains):
            carry, acc = results[c][0]
            acc_ref[c] = acc
            carry_ref[c] = carry
            more = (blk_i[u] >= 2) & (jnp.min(carry) < EXIT_LOG2)
            needs_more = more if needs_more is None else needs_more | more

        @pl.when(needs_more)
        def _():
            def per_chain(c, _):
                u = c // n_pairs
                p = c % n_pairs
                i = blk_i[0] + u
                q2 = q_ref[0, p, pl.ds(pl.multiple_of((first + u) * t, t), t), :]
                qs = jnp.concatenate(
                    [jnp.where((lane >= h * HEAD_DIM) & (lane < (h + 1) * HEAD_DIM), q2, jnp.zeros_like(q2))
                     for h in range(n_heads)], axis=0)

                def more_blocks(state):
                    n, chain_min, _, _ = state
                    return (n < i // 2) & (chain_min < EXIT_LOG2)

                def pair_body(state):
                    n, _, carry, acc = state
                    j0 = i - 2 - 2 * n
                    result = []
                    for _ in pair_stages(p, qs, (carry, acc), j0, jnp.maximum(j0 - 1, 0), j0 >= 1, result):
                        pass
                    carry, acc = result[0]
                    return n + 1, jnp.min(carry), carry, acc

                carry = carry_ref[c]
                _, _, _, acc = lax.while_loop(more_blocks, pair_body,
                                              (0, jnp.min(carry), carry, acc_ref[c]))
                acc_ref[c] = acc
                return 0

            lax.fori_loop(0, len(chains), per_chain, 0)

        for u in range(BLOCKS_PER_GROUP):
            y = jnp.concatenate(
                [jnp.where(lane < HEAD_DIM, acc_ref[u * n_pairs + p, 0:t], acc_ref[u * n_pairs + p, t:2 * t])
                 for p in range(n_pairs)], axis=-1)
            o_ref[0, q_rows(u), :] = _rms(y, g_ref[...]).astype(BF16)
        return 0

    lax.fori_loop(0, blocks_per_step // BLOCKS_PER_GROUP, query_group, 0)


def _stickbreak(q, k, v, attn_out_g, *, t, blocks_per_step):
    B, n_pairs, S, _ = q.shape
    d_attn = n_pairs * LANES
    rows = (LANES // HEAD_DIM) * t
    ts = t * blocks_per_step
    kern = functools.partial(_stickbreak_kernel, t=t, blocks_per_step=blocks_per_step)
    full = lambda b, i: (b, 0, 0, 0)
    return pl.pallas_call(
        kern,
        grid=(B, S // ts),
        in_specs=[
            pl.BlockSpec((1, n_pairs, ts, LANES), lambda b, i: (b, 0, i, 0)),
            pl.BlockSpec((1, n_pairs, S, LANES), full),
            pl.BlockSpec((1, n_pairs, S, LANES), full),
            pl.BlockSpec((1, d_attn), lambda b, i: (0, 0)),
        ],
        out_specs=pl.BlockSpec((1, ts, d_attn), lambda b, i: (b, i, 0)),
        out_shape=jax.ShapeDtypeStruct((B, S, d_attn), BF16),
        scratch_shapes=[pltpu.VMEM((BLOCKS_PER_GROUP * n_pairs, rows, LANES), F32),
                        pltpu.VMEM((BLOCKS_PER_GROUP * n_pairs, rows, 1), F32)],
        compiler_params=pltpu.CompilerParams(
            dimension_semantics=("arbitrary", "arbitrary"),
            vmem_limit_bytes=VMEM_LIMIT_BYTES),
        name="stickbreak",
    )(q, k, v, attn_out_g.reshape(1, d_attn))


def _out_mlp_kernel(x_ref, pool_ref, attn_ref, w_out_ref, g2_ref, w_up_ref, w_down_ref, gf_ref, o_ref, *,
                    ff_chunk):
    mixed = jnp.concatenate([pool_ref[...], attn_ref[...]], axis=-1)
    h = x_ref[...] + jnp.dot(mixed, w_out_ref[...], preferred_element_type=F32)

    hn2 = _rms(h, g2_ref[...]).astype(BF16)
    d_ff = w_up_ref.shape[1]
    acc = h
    for c in range(d_ff // ff_chunk):
        up = jnp.dot(hn2, w_up_ref[:, c * ff_chunk:(c + 1) * ff_chunk], preferred_element_type=F32)
        act = jnp.square(jnp.maximum(up, 0.0)).astype(BF16)
        acc = acc + jnp.dot(act, w_down_ref[c * ff_chunk:(c + 1) * ff_chunk, :],
                            preferred_element_type=F32)
    o_ref[...] = _rms(acc, gf_ref[...])


def _out_mlp(x2, pool_n, attn_n, w_out, norm2_g, w_up, w_down, final_g, *, tm, ff_chunk):
    T, D = x2.shape
    d_pool = pool_n.shape[1]
    d_attn = attn_n.shape[1]
    d_ff = w_up.shape[1]
    kern = functools.partial(_out_mlp_kernel, ff_chunk=ff_chunk)
    tok = lambda i: (i, 0)
    const = lambda i: (0, 0)
    resident = functools.partial(pl.BlockSpec, index_map=const, pipeline_mode=pl.Buffered(1))
    return pl.pallas_call(
        kern,
        grid=(T // tm,),
        in_specs=[
            pl.BlockSpec((tm, D), tok),
            pl.BlockSpec((tm, d_pool), tok),
            pl.BlockSpec((tm, d_attn), tok),
            resident((d_pool + d_attn, D)),
            pl.BlockSpec((1, D), const),
            resident((D, d_ff)),
            resident((d_ff, D)),
            pl.BlockSpec((1, D), const),
        ],
        out_specs=pl.BlockSpec((tm, D), tok),
        out_shape=jax.ShapeDtypeStruct((T, D), F32),
        compiler_params=pltpu.CompilerParams(
            dimension_semantics=("arbitrary",),
            vmem_limit_bytes=VMEM_LIMIT_BYTES),
        name="out_mlp",
    )(x2, pool_n, attn_n, w_out, norm2_g.reshape(1, D), w_up, w_down, final_g.reshape(1, D))


def kernel(x, norm1_g, w_in, pool_w, pool_scale, pool_out_g, attn_out_g, w_out, norm2_g, w_up, w_down,
           final_g):
    B, S, D = x.shape
    x2 = x.reshape(B * S, D)
    pool_n, q, k, v, (w_out_bf, w_up_bf, w_down_bf) = _in_proj_pool(
        x2, S, norm1_g, w_in, pool_w, pool_scale, pool_out_g, [(w_out, 0), (w_up, 1), (w_down, 0)], tm=512)
    attn_n = _stickbreak(q, k, v, attn_out_g, t=256, blocks_per_step=4)
    out = _out_mlp(x2, pool_n, attn_n.reshape(B * S, -1), w_out_bf, norm2_g, w_up_bf, w_down_bf, final_g,
                   tm=1024, ff_chunk=512)
    return out.reshape(B, S, D)
```

```python
import functools
import math

import jax
import jax.numpy as jnp
from jax import lax
from jax.experimental import pallas as pl
from jax.experimental.pallas import tpu as pltpu

EPS = 1e-6
LOG2_E = math.log2(math.e)
POOL_WINDOWS = (2, 4, 8, 16)
HEAD_DIM = 64
LANES = 128
MASKED_SCORE = -1e30
EXIT_LOG2 = 152.0
POOL_HALO = 16
POOL_PAD = 8
BF16_SUBLANES = 16
VMEM_LIMIT_BYTES = 56 * 1024 * 1024

MXU_DIM = 256
PROJ_ROWS = 2 * MXU_DIM
ATTN_BLOCK = MXU_DIM
BLOCKS_PER_GROUP = 4
ATTN_BLOCKS_PER_STEP = 2 * BLOCKS_PER_GROUP
MLP_ROWS = 4 * MXU_DIM
FF_CHUNK = 2 * MXU_DIM

BF16 = jnp.bfloat16
F32 = jnp.float32


def _rms(x, g):
    r = lax.rsqrt(jnp.mean(x * x, axis=-1, keepdims=True) + EPS)
    return x * r * g


def _in_proj_pool_kernel(x_ref, g1_ref, w_in_ref, pool_w_ref, pool_scale_ref, pool_g_ref, later_w_refs,
                         pool_out_ref, q_ref, k_ref, v_ref, later_w_bf_refs,
                         w_bf_ref, ubuf_ref, s2_ref, s4_ref, s8_ref, *,
                         tm, tiles_per_seq, n_tiles, d_pool, d_attn):
    g = pl.program_id(0)
    gdim = d_pool // len(POOL_WINDOWS)
    off = POOL_PAD + POOL_HALO
    levels = (ubuf_ref, s2_ref, s4_ref, s8_ref)

    @pl.when(g == 0)
    def _():
        w_bf_ref[...] = w_in_ref[...].astype(BF16)
        ubuf_ref[...] = jnp.zeros_like(ubuf_ref)
        for lvl in levels[1:]:
            lvl[0:POOL_PAD, :] = jnp.zeros((POOL_PAD, lvl.shape[1]), F32)

    for w_ref, w_bf in zip(later_w_refs, later_w_bf_refs):
        w_bf[...] = w_ref[...].astype(BF16)

    hn = _rms(x_ref[...], g1_ref[...]).astype(BF16)
    n_pairs = d_attn // LANES

    def project(first_col, width):
        return jnp.dot(hn, w_bf_ref[:, first_col:first_col + width], preferred_element_type=F32)

    def emit_heads(out_ref, first_col, scale=None):
        cols = project(first_col, d_attn)
        for p in range(n_pairs):
            blk = cols[:, p * LANES:(p + 1) * LANES]
            out_ref[0, p] = (blk if scale is None else blk * scale).astype(BF16)

    sums = [None] * len(POOL_WINDOWS)

    def window_level(n):
        shift = POOL_WINDOWS[n] // 2
        src = levels[n]
        if n + 1 < len(levels):
            doubled = src[POOL_PAD:off + tm, :] + src[POOL_PAD - shift:off + tm - shift, :]
            sums[n] = doubled[POOL_HALO:, 0:gdim]
            levels[n + 1][POOL_PAD:off + tm, :] = doubled[:, gdim:]
        else:
            sums[n] = src[off:off + tm, :] + src[off - shift:off + tm - shift, :]

    emit_heads(q_ref, d_pool, LOG2_E * HEAD_DIM ** -0.5)
    window_level(0)
    window_level(1)
    emit_heads(k_ref, d_pool + d_attn)
    window_level(2)
    window_level(3)
    emit_heads(v_ref, d_pool + 2 * d_attn)

    seq_tile = jnp.maximum(g - 1, 0) % tiles_per_seq
    head_pos = seq_tile * tm + lax.broadcasted_iota(jnp.int32, (POOL_HALO, 1), 0)
    mapped = []
    for grp, w in enumerate(POOL_WINDOWS):
        head_scale = w / jnp.minimum(head_pos + 1, w).astype(F32)
        mean = jnp.concatenate([sums[grp][0:POOL_HALO] * head_scale, sums[grp][POOL_HALO:]], axis=0)
        mean = mean * (1.0 / w)
        pooled = mean - ubuf_ref[off:off + tm, grp * gdim:(grp + 1) * gdim]
        mapped.append(jnp.dot(pooled.astype(BF16), pool_w_ref[grp].astype(BF16),
                              preferred_element_type=F32))
    y = jnp.concatenate(mapped, axis=-1) * pool_scale_ref[...]
    pool_out_ref[...] = _rms(y, pool_g_ref[...]).astype(BF16)

    starts_seq = jnp.minimum(g, n_tiles - 1) % tiles_per_seq == 0
    ubuf_ref[POOL_PAD:off, :] = jnp.where(starts_seq, 0.0, ubuf_ref[tm + POOL_PAD:tm + off, :])

    ubuf_ref[off:off + tm, :] = project(0, d_pool)


def _in_proj_pool(x2, seq_len, norm1_g, w_in, pool_w, pool_scale, pool_out_g, later_weights, *, tm):
    T, D = x2.shape
    n_groups, gdim, _ = pool_w.shape
    d_pool = n_groups * gdim
    d_attn = (w_in.shape[1] - d_pool) // 3
    tiles_per_seq = seq_len // tm
    n_tiles = T // tm
    n_pairs = d_attn // LANES
    kern = functools.partial(_in_proj_pool_kernel, tm=tm, tiles_per_seq=tiles_per_seq, n_tiles=n_tiles,
                             d_pool=d_pool, d_attn=d_attn)

    def proj_tile(g):
        return jnp.minimum(g, n_tiles - 1)

    def slab_spec(w, axis):
        shape = list(w.shape)
        shape[axis] //= n_tiles
        return pl.BlockSpec(tuple(shape), lambda g: tuple(proj_tile(g) if a == axis else 0 for a in range(2)))

    for w, axis in later_weights:
        assert w.ndim == 2 and w.shape[axis] % (n_tiles * BF16_SUBLANES) == 0, (w.shape, axis, n_tiles)
    slab_specs = [slab_spec(w, axis) for w, axis in later_weights]

    const2 = lambda g: (0, 0)
    qkv_spec = pl.BlockSpec((1, n_pairs, tm, LANES),
                            lambda g: (proj_tile(g) // tiles_per_seq, 0, proj_tile(g) % tiles_per_seq, 0))
    qkv_shape = jax.ShapeDtypeStruct((T // seq_len, n_pairs, seq_len, LANES), BF16)
    return pl.pallas_call(
        kern,
        grid=(n_tiles + 1,),
        in_specs=[
            pl.BlockSpec((tm, D), lambda g: (proj_tile(g), 0)),
            pl.BlockSpec((1, D), const2),
            pl.BlockSpec(w_in.shape, const2, pipeline_mode=pl.Buffered(1)),
            pl.BlockSpec(pool_w.shape, lambda g: (0, 0, 0)),
            pl.BlockSpec((1, d_pool), const2),
            pl.BlockSpec((1, d_pool), const2),
            slab_specs,
        ],
        out_specs=[pl.BlockSpec((tm, d_pool), lambda g: (jnp.maximum(g - 1, 0), 0)),
                   qkv_spec, qkv_spec, qkv_spec, slab_specs],
        out_shape=[jax.ShapeDtypeStruct((T, d_pool), BF16), qkv_shape, qkv_shape, qkv_shape,
                   [jax.ShapeDtypeStruct(w.shape, BF16) for w, _ in later_weights]],
        scratch_shapes=[pltpu.VMEM(w_in.shape, BF16)] +
                       [pltpu.VMEM((POOL_PAD + POOL_HALO + tm, d_pool - n * gdim), F32)
                        for n in range(n_groups)],
        compiler_params=pltpu.CompilerParams(
            dimension_semantics=("arbitrary",),
            vmem_limit_bytes=VMEM_LIMIT_BYTES),
        name="in_proj_pool",
    )(x2, norm1_g.reshape(1, D), w_in, pool_w, pool_scale.reshape(1, d_pool),
      pool_out_g.reshape(1, d_pool), [w for w, _ in later_weights])


def _stickbreak_kernel(q_ref, k_ref, v_ref, g_ref, o_ref, acc_ref, carry_ref, *, t, blocks_per_step):
    n_pairs = q_ref.shape[1]
    n_heads = LANES // HEAD_DIM
    rows = n_heads * t
    lane = lax.broadcasted_iota(jnp.int32, (t, LANES), 1)
    key_r = lax.broadcasted_iota(jnp.int32, (t, t), 0)
    key_c = lax.broadcasted_iota(jnp.int32, (t, t), 1)
    later_or_same = (key_r >= key_c).astype(BF16)
    qrow = lax.broadcasted_iota(jnp.int32, (rows, t), 0) & (t - 1)
    causal = lax.broadcasted_iota(jnp.int32, (rows, t), 1) < qrow

    def query_group(grp, _):
        first = grp * BLOCKS_PER_GROUP
        blk_i = [pl.program_id(1) * blocks_per_step + first + u for u in range(BLOCKS_PER_GROUP)]

        def q_rows(u):
            return pl.ds(pl.multiple_of((first + u) * t, t), t)

        def stacked_q(u, p):
            q2 = q_ref[0, p, q_rows(u), :]
            return jnp.concatenate(
                [jnp.where((lane >= h * HEAD_DIM) & (lane < (h + 1) * HEAD_DIM), q2, jnp.zeros_like(q2))
                 for h in range(n_heads)], axis=0)

        def pair_stages(p, qs, state, j0, j1, has_second, result):
            js = (j0, j1)
            zs, sps = [], []
            for n, j in enumerate(js):
                kb = k_ref[0, p, pl.ds(j * t, t), :]
                z = lax.dot_general(qs, kb, (((1,), (1,)), ((), ())), preferred_element_type=F32)
                if state is None and n == 0:
                    z = jnp.where(causal, z, MASKED_SCORE)
                zs.append(z)
                sps.append(jnp.maximum(z, 0.0) + jnp.log2(1.0 + jnp.exp2(-jnp.abs(z))))
            yield
            sp_all = jnp.concatenate(sps, axis=0)
            from_here_all = jnp.dot(sp_all.astype(BF16), later_or_same, preferred_element_type=F32)
            yield
            pvs, block_sums = [], []
            for n, j in enumerate(js):
                from_here = from_here_all[n * rows:(n + 1) * rows]
                a = jnp.exp2(zs[n] - from_here)
                vb = v_ref[0, p, pl.ds(j * t, t), :]
                pvs.append(jnp.dot(a.astype(BF16), vb, preferred_element_type=F32))
                block_sums.append(from_here[:, 0:1])
            if state is None:
                carry, acc = block_sums[0], pvs[0]
            else:
                carry, acc = state
                acc = acc + jnp.exp2(-carry) * pvs[0]
                carry = carry + block_sums[0]
            acc = acc + jnp.where(has_second, jnp.exp2(-carry), 0.0) * pvs[1]
            carry = carry + jnp.where(has_second, block_sums[1], 0.0)
            result.append((carry, acc))

        chains = [(u, p) for u in range(BLOCKS_PER_GROUP) for p in range(n_pairs)]
        results = [[] for _ in chains]
        stages = [pair_stages(p, stacked_q(u, p), None, blk_i[u], jnp.maximum(blk_i[u] - 1, 0),
                              blk_i[u] >= 1, results[c]) for c, (u, p) in enumerate(chains)]
        n_stages = 3
        for tick in range(len(chains) + n_stages - 1):
            for c in range(len(chains)):
                if 0 <= tick - c < n_stages:
                    next(stages[c], None)

        needs_more = None
        for c, (u, p) in enumerate(chains):
            carry, acc = results[c][0]
            acc_ref[c] = acc
            carry_ref[c] = carry
            more = (blk_i[u] >= 2) & (jnp.min(carry) < EXIT_LOG2)
            needs_more = more if needs_more is None else needs_more | more

        @pl.when(needs_more)
        def _():
            def per_chain(c, _):
                u = c // n_pairs
                p = c % n_pairs
                i = blk_i[0] + u
                qs = stacked_q(u, p)

                def more_blocks(state):
                    n, chain_min, _, _ = state
                    return (n < i // 2) & (chain_min < EXIT_LOG2)

                def pair_body(state):
                    n, _, carry, acc = state
                    j0 = i - 2 - 2 * n
                    result = []
                    for _ in pair_stages(p, qs, (carry, acc), j0, jnp.maximum(j0 - 1, 0), j0 >= 1, result):
                        pass
                    carry, acc = result[0]
                    return n + 1, jnp.min(carry), carry, acc

                carry = carry_ref[c]
                _, _, _, acc = lax.while_loop(more_blocks, pair_body,
                                              (0, jnp.min(carry), carry, acc_ref[c]))
                acc_ref[c] = acc
                return 0

            lax.fori_loop(0, len(chains), per_chain, 0)

        for u in range(BLOCKS_PER_GROUP):
            y = jnp.concatenate(
                [jnp.where(lane < HEAD_DIM, acc_ref[u * n_pairs + p, 0:t], acc_ref[u * n_pairs + p, t:2 * t])
                 for p in range(n_pairs)], axis=-1)
            o_ref[0, q_rows(u), :] = _rms(y, g_ref[...]).astype(BF16)
        return 0

    lax.fori_loop(0, blocks_per_step // BLOCKS_PER_GROUP, query_group, 0)


def _stickbreak(q, k, v, attn_out_g, *, t, blocks_per_step):
    B, n_pairs, S, _ = q.shape
    d_attn = n_pairs * LANES
    rows = (LANES // HEAD_DIM) * t
    ts = t * blocks_per_step
    kern = functools.partial(_stickbreak_kernel, t=t, blocks_per_step=blocks_per_step)
    full = lambda b, i: (b, 0, 0, 0)
    return pl.pallas_call(
        kern,
        grid=(B, S // ts),
        in_specs=[
            pl.BlockSpec((1, n_pairs, ts, LANES), lambda b, i: (b, 0, i, 0)),
            pl.BlockSpec((1, n_pairs, S, LANES), full),
            pl.BlockSpec((1, n_pairs, S, LANES), full),
            pl.BlockSpec((1, d_attn), lambda b, i: (0, 0)),
        ],
        out_specs=pl.BlockSpec((1, ts, d_attn), lambda b, i: (b, i, 0)),
        out_shape=jax.ShapeDtypeStruct((B, S, d_attn), BF16),
        scratch_shapes=[pltpu.VMEM((BLOCKS_PER_GROUP * n_pairs, rows, LANES), F32),
                        pltpu.VMEM((BLOCKS_PER_GROUP * n_pairs, rows, 1), F32)],
        compiler_params=pltpu.CompilerParams(
            dimension_semantics=("arbitrary", "arbitrary"),
            vmem_limit_bytes=VMEM_LIMIT_BYTES),
        name="stickbreak",
    )(q, k, v, attn_out_g.reshape(1, d_attn))


def _out_mlp_kernel(x_ref, pool_ref, attn_ref, w_out_ref, g2_ref, w_up_ref, w_down_ref, gf_ref, o_ref, *,
                    ff_chunk):
    mixed = jnp.concatenate([pool_ref[...], attn_ref[...]], axis=-1)
    h = x_ref[...] + jnp.dot(mixed, w_out_ref[...], preferred_element_type=F32)

    hn2 = _rms(h, g2_ref[...]).astype(BF16)
    d_ff = w_up_ref.shape[1]
    acc = h
    for c in range(d_ff // ff_chunk):
        up = jnp.dot(hn2, w_up_ref[:, c * ff_chunk:(c + 1) * ff_chunk], preferred_element_type=F32)
        act = jnp.square(jnp.maximum(up, 0.0)).astype(BF16)
        acc = acc + jnp.dot(act, w_down_ref[c * ff_chunk:(c + 1) * ff_chunk, :],
                            preferred_element_type=F32)
    o_ref[...] = _rms(acc, gf_ref[...])


def _out_mlp(x2, pool_n, attn_n, w_out, norm2_g, w_up, w_down, final_g, *, tm, ff_chunk):
    T, D = x2.shape
    d_pool = pool_n.shape[1]
    d_attn = attn_n.shape[1]
    d_ff = w_up.shape[1]
    kern = functools.partial(_out_mlp_kernel, ff_chunk=ff_chunk)
    tok = lambda i: (i, 0)
    const = lambda i: (0, 0)
    resident = functools.partial(pl.BlockSpec, index_map=const, pipeline_mode=pl.Buffered(1))
    return pl.pallas_call(
        kern,
        grid=(T // tm,),
        in_specs=[
            pl.BlockSpec((tm, D), tok),
            pl.BlockSpec((tm, d_pool), tok),
            pl.BlockSpec((tm, d_attn), tok),
            resident((d_pool + d_attn, D)),
            pl.BlockSpec((1, D), const),
            resident((D, d_ff)),
            resident((d_ff, D)),
            pl.BlockSpec((1, D), const),
        ],
        out_specs=pl.BlockSpec((tm, D), tok),
        out_shape=jax.ShapeDtypeStruct((T, D), F32),
        compiler_params=pltpu.CompilerParams(
            dimension_semantics=("arbitrary",),
            vmem_limit_bytes=VMEM_LIMIT_BYTES),
        name="out_mlp",
    )(x2, pool_n, attn_n, w_out, norm2_g.reshape(1, D), w_up, w_down, final_g.reshape(1, D))


def kernel(x, norm1_g, w_in, pool_w, pool_scale, pool_out_g, attn_out_g, w_out, norm2_g, w_up, w_down,
           final_g):
    B, S, D = x.shape
    assert S % PROJ_ROWS == 0 and S % (ATTN_BLOCK * ATTN_BLOCKS_PER_STEP) == 0 and (B * S) % MLP_ROWS == 0
    assert w_up.shape[1] % FF_CHUNK == 0
    x2 = x.reshape(B * S, D)
    pool_n, q, k, v, (w_out_bf, w_up_bf, w_down_bf) = _in_proj_pool(
        x2, S, norm1_g, w_in, pool_w, pool_scale, pool_out_g, [(w_out, 0), (w_up, 1), (w_down, 0)],
        tm=PROJ_ROWS)
    attn_n = _stickbreak(q, k, v, attn_out_g, t=ATTN_BLOCK, blocks_per_step=ATTN_BLOCKS_PER_STEP)
    out = _out_mlp(x2, pool_n, attn_n.reshape(B * S, -1), w_out_bf, norm2_g, w_up_bf, w_down_bf, final_g,
                   tm=MLP_ROWS, ff_chunk=FF_CHUNK)
    return out.reshape(B, S, D)
```

```python
import functools
import math

import jax
import jax.numpy as jnp
from jax import lax
from jax.experimental import pallas as pl
from jax.experimental.pallas import tpu as pltpu

EPS = 1e-6
LOG2_E = math.log2(math.e)
POOL_WINDOWS = (2, 4, 8, 16)
HEAD_DIM = 64
LANES = 128
MASKED_SCORE = -1e30
EXIT_LOG2 = 152.0
POOL_HALO = 16
POOL_PAD = 8
BF16_SUBLANES = 16
VMEM_LIMIT_BYTES = 56 * 1024 * 1024

MXU_DIM = 256
PROJ_ROWS = 2 * MXU_DIM
ATTN_BLOCK = MXU_DIM
BLOCKS_PER_GROUP = 4
ATTN_BLOCKS_PER_STEP = BLOCKS_PER_GROUP
MLP_ROWS = 4 * MXU_DIM
FF_CHUNK = 2 * MXU_DIM

BF16 = jnp.bfloat16
F32 = jnp.float32


def _rms(x, g):
    r = lax.rsqrt(jnp.mean(x * x, axis=-1, keepdims=True) + EPS)
    return x * r * g


def _in_proj_pool_kernel(x_ref, g1_ref, w_in_ref, pool_w_ref, pool_scale_ref, pool_g_ref, later_w_refs,
                         pool_out_ref, q_ref, k_ref, v_ref, later_w_bf_refs,
                         w_bf_ref, ubuf_ref, s2_ref, s4_ref, s8_ref, *,
                         tm, tiles_per_seq, n_tiles, d_pool, d_attn):
    g = pl.program_id(0)
    gdim = d_pool // len(POOL_WINDOWS)
    off = POOL_PAD + POOL_HALO
    levels = (ubuf_ref, s2_ref, s4_ref, s8_ref)

    @pl.when(g == 0)
    def _():
        w_bf_ref[...] = w_in_ref[...].astype(BF16)
        ubuf_ref[...] = jnp.zeros_like(ubuf_ref)
        for lvl in levels[1:]:
            lvl[0:POOL_PAD, :] = jnp.zeros((POOL_PAD, lvl.shape[1]), F32)

    for w_ref, w_bf in zip(later_w_refs, later_w_bf_refs):
        w_bf[...] = w_ref[...].astype(BF16)

    hn = _rms(x_ref[...], g1_ref[...]).astype(BF16)
    n_pairs = d_attn // LANES

    def project(first_col, width):
        return jnp.dot(hn, w_bf_ref[:, first_col:first_col + width], preferred_element_type=F32)

    def emit_heads(out_ref, first_col, scale=None):
        cols = project(first_col, d_attn)
        for p in range(n_pairs):
            blk = cols[:, p * LANES:(p + 1) * LANES]
            out_ref[0, p] = (blk if scale is None else blk * scale).astype(BF16)

    sums = [None] * len(POOL_WINDOWS)

    def window_level(n):
        shift = POOL_WINDOWS[n] // 2
        src = levels[n]
        if n + 1 < len(levels):
            doubled = src[POOL_PAD:off + tm, :] + src[POOL_PAD - shift:off + tm - shift, :]
            sums[n] = doubled[POOL_HALO:, 0:gdim]
            levels[n + 1][POOL_PAD:off + tm, :] = doubled[:, gdim:]
        else:
            sums[n] = src[off:off + tm, :] + src[off - shift:off + tm - shift, :]

    emit_heads(q_ref, d_pool, LOG2_E * HEAD_DIM ** -0.5)
    window_level(0)
    window_level(1)
    emit_heads(k_ref, d_pool + d_attn)
    window_level(2)
    window_level(3)
    emit_heads(v_ref, d_pool + 2 * d_attn)

    seq_tile = jnp.maximum(g - 1, 0) % tiles_per_seq
    head_pos = seq_tile * tm + lax.broadcasted_iota(jnp.int32, (POOL_HALO, 1), 0)
    mapped = []
    for grp, w in enumerate(POOL_WINDOWS):
        head_scale = w / jnp.minimum(head_pos + 1, w).astype(F32)
        mean = jnp.concatenate([sums[grp][0:POOL_HALO] * head_scale, sums[grp][POOL_HALO:]], axis=0)
        mean = mean * (1.0 / w)
        pooled = mean - ubuf_ref[off:off + tm, grp * gdim:(grp + 1) * gdim]
        mapped.append(jnp.dot(pooled.astype(BF16), pool_w_ref[grp].astype(BF16),
                              preferred_element_type=F32))
    y = jnp.concatenate(mapped, axis=-1) * pool_scale_ref[...]
    pool_out_ref[...] = _rms(y, pool_g_ref[...]).astype(BF16)

    starts_seq = jnp.minimum(g, n_tiles - 1) % tiles_per_seq == 0
    ubuf_ref[POOL_PAD:off, :] = jnp.where(starts_seq, 0.0, ubuf_ref[tm + POOL_PAD:tm + off, :])

    ubuf_ref[off:off + tm, :] = project(0, d_pool)


def _in_proj_pool(x2, seq_len, norm1_g, w_in, pool_w, pool_scale, pool_out_g, later_weights, *, tm):
    T, D = x2.shape
    n_groups, gdim, _ = pool_w.shape
    d_pool = n_groups * gdim
    d_attn = (w_in.shape[1] - d_pool) // 3
    tiles_per_seq = seq_len // tm
    n_tiles = T // tm
    n_pairs = d_attn // LANES
    kern = functools.partial(_in_proj_pool_kernel, tm=tm, tiles_per_seq=tiles_per_seq, n_tiles=n_tiles,
                             d_pool=d_pool, d_attn=d_attn)

    def proj_tile(g):
        return jnp.minimum(g, n_tiles - 1)

    def slab_spec(w, axis):
        shape = list(w.shape)
        shape[axis] //= n_tiles
        return pl.BlockSpec(tuple(shape), lambda g: tuple(proj_tile(g) if a == axis else 0 for a in range(2)))

    for w, axis in later_weights:
        assert w.ndim == 2 and w.shape[axis] % (n_tiles * BF16_SUBLANES) == 0, (w.shape, axis, n_tiles)
    slab_specs = [slab_spec(w, axis) for w, axis in later_weights]

    const2 = lambda g: (0, 0)
    qkv_spec = pl.BlockSpec((1, n_pairs, tm, LANES),
                            lambda g: (proj_tile(g) // tiles_per_seq, 0, proj_tile(g) % tiles_per_seq, 0))
    qkv_shape = jax.ShapeDtypeStruct((T // seq_len, n_pairs, seq_len, LANES), BF16)
    return pl.pallas_call(
        kern,
        grid=(n_tiles + 1,),
        in_specs=[
            pl.BlockSpec((tm, D), lambda g: (proj_tile(g), 0)),
            pl.BlockSpec((1, D), const2),
            pl.BlockSpec(w_in.shape, const2, pipeline_mode=pl.Buffered(1)),
            pl.BlockSpec(pool_w.shape, lambda g: (0, 0, 0)),
            pl.BlockSpec((1, d_pool), const2),
            pl.BlockSpec((1, d_pool), const2),
            slab_specs,
        ],
        out_specs=[pl.BlockSpec((tm, d_pool), lambda g: (jnp.maximum(g - 1, 0), 0)),
                   qkv_spec, qkv_spec, qkv_spec, slab_specs],
        out_shape=[jax.ShapeDtypeStruct((T, d_pool), BF16), qkv_shape, qkv_shape, qkv_shape,
                   [jax.ShapeDtypeStruct(w.shape, BF16) for w, _ in later_weights]],
        scratch_shapes=[pltpu.VMEM(w_in.shape, BF16)] +
                       [pltpu.VMEM((POOL_PAD + POOL_HALO + tm, d_pool - n * gdim), F32)
                        for n in range(n_groups)],
        compiler_params=pltpu.CompilerParams(
            dimension_semantics=("arbitrary",),
            vmem_limit_bytes=VMEM_LIMIT_BYTES),
        name="in_proj_pool",
    )(x2, norm1_g.reshape(1, D), w_in, pool_w, pool_scale.reshape(1, d_pool),
      pool_out_g.reshape(1, d_pool), [w for w, _ in later_weights])


def _stickbreak_kernel(q_ref, k_ref, v_ref, g_ref, o_ref, acc_ref, carry_ref, *, t, blocks_per_step):
    n_pairs = q_ref.shape[1]
    n_heads = LANES // HEAD_DIM
    rows = n_heads * t
    lane = lax.broadcasted_iota(jnp.int32, (t, LANES), 1)
    key_r = lax.broadcasted_iota(jnp.int32, (t, t), 0)
    key_c = lax.broadcasted_iota(jnp.int32, (t, t), 1)
    later_or_same = (key_r >= key_c).astype(BF16)
    qrow = lax.broadcasted_iota(jnp.int32, (rows, t), 0) & (t - 1)
    causal = lax.broadcasted_iota(jnp.int32, (rows, t), 1) < qrow

    def query_group(grp, _):
        first = grp * BLOCKS_PER_GROUP
        blk_i = [pl.program_id(1) * blocks_per_step + first + u for u in range(BLOCKS_PER_GROUP)]

        def q_rows(u):
            return pl.ds(pl.multiple_of((first + u) * t, t), t)

        def stacked_q(u, p):
            q2 = q_ref[0, p, q_rows(u), :]
            return jnp.concatenate(
                [jnp.where((lane >= h * HEAD_DIM) & (lane < (h + 1) * HEAD_DIM), q2, jnp.zeros_like(q2))
                 for h in range(n_heads)], axis=0)

        def pair_stages(p, qs, state, j0, j1, has_second, result):
            js = (j0, j1)
            zs, sps = [], []
            for n, j in enumerate(js):
                kb = k_ref[0, p, pl.ds(j * t, t), :]
                z = lax.dot_general(qs, kb, (((1,), (1,)), ((), ())), preferred_element_type=F32)
                if state is None and n == 0:
                    z = jnp.where(causal, z, MASKED_SCORE)
                zs.append(z)
                sps.append(jnp.maximum(z, 0.0) + jnp.log2(1.0 + jnp.exp2(-jnp.abs(z))))
            yield
            sp_all = jnp.concatenate(sps, axis=0)
            from_here_all = jnp.dot(sp_all.astype(BF16), later_or_same, preferred_element_type=F32)
            yield
            pvs, block_sums = [], []
            for n, j in enumerate(js):
                from_here = from_here_all[n * rows:(n + 1) * rows]
                a = jnp.exp2(zs[n] - from_here)
                vb = v_ref[0, p, pl.ds(j * t, t), :]
                pvs.append(jnp.dot(a.astype(BF16), vb, preferred_element_type=F32))
                block_sums.append(from_here[:, 0:1])
            if state is None:
                carry, acc = block_sums[0], pvs[0]
            else:
                carry, acc = state
                acc = acc + jnp.exp2(-carry) * pvs[0]
                carry = carry + block_sums[0]
            acc = acc + jnp.where(has_second, jnp.exp2(-carry), 0.0) * pvs[1]
            carry = carry + jnp.where(has_second, block_sums[1], 0.0)
            result.append((carry, acc))

        chains = [(u, p) for u in range(BLOCKS_PER_GROUP) for p in range(n_pairs)]
        results = [[] for _ in chains]
        stages = [pair_stages(p, stacked_q(u, p), None, blk_i[u], jnp.maximum(blk_i[u] - 1, 0),
                              blk_i[u] >= 1, results[c]) for c, (u, p) in enumerate(chains)]
        n_stages = 3
        for tick in range(len(chains) + n_stages - 1):
            for c in range(len(chains)):
                if 0 <= tick - c < n_stages:
                    next(stages[c], None)

        needs_more = None
        for c, (u, p) in enumerate(chains):
            carry, acc = results[c][0]
            acc_ref[c] = acc
            carry_ref[c] = carry
            more = (blk_i[u] >= 2) & (jnp.min(carry) < EXIT_LOG2)
            needs_more = more if needs_more is None else needs_more | more

        @pl.when(needs_more)
        def _():
            def per_chain(c, _):
                u = c // n_pairs
                p = c % n_pairs
                i = blk_i[0] + u
                qs = stacked_q(u, p)

                def more_blocks(state):
                    n, chain_min, _, _ = state
                    return (n < i // 2) & (chain_min < EXIT_LOG2)

                def pair_body(state):
                    n, _, carry, acc = state
                    j0 = i - 2 - 2 * n
                    result = []
                    for _ in pair_stages(p, qs, (carry, acc), j0, jnp.maximum(j0 - 1, 0), j0 >= 1, result):
                        pass
                    carry, acc = result[0]
                    return n + 1, jnp.min(carry), carry, acc

                carry = carry_ref[c]
                _, _, _, acc = lax.while_loop(more_blocks, pair_body,
                                              (0, jnp.min(carry), carry, acc_ref[c]))
                acc_ref[c] = acc
                return 0

            lax.fori_loop(0, len(chains), per_chain, 0)

        for u in range(BLOCKS_PER_GROUP):
            y = jnp.concatenate(
                [jnp.where(lane < HEAD_DIM, acc_ref[u * n_pairs + p, 0:t], acc_ref[u * n_pairs + p, t:2 * t])
                 for p in range(n_pairs)], axis=-1)
            o_ref[0, q_rows(u), :] = _rms(y, g_ref[...]).astype(BF16)
        return 0

    lax.fori_loop(0, blocks_per_step // BLOCKS_PER_GROUP, query_group, 0)


def _stickbreak(q, k, v, attn_out_g, *, t, blocks_per_step):
    B, n_pairs, S, _ = q.shape
    d_attn = n_pairs * LANES
    rows = (LANES // HEAD_DIM) * t
    ts = t * blocks_per_step
    kern = functools.partial(_stickbreak_kernel, t=t, blocks_per_step=blocks_per_step)
    full = lambda b, i: (b, 0, 0, 0)
    return pl.pallas_call(
        kern,
        grid=(B, S // ts),
        in_specs=[
            pl.BlockSpec((1, n_pairs, ts, LANES), lambda b, i: (b, 0, i, 0)),
            pl.BlockSpec((1, n_pairs, S, LANES), full),
            pl.BlockSpec((1, n_pairs, S, LANES), full),
            pl.BlockSpec((1, d_attn), lambda b, i: (0, 0)),
        ],
        out_specs=pl.BlockSpec((1, ts, d_attn), lambda b, i: (b, i, 0)),
        out_shape=jax.ShapeDtypeStruct((B, S, d_attn), BF16),
        scratch_shapes=[pltpu.VMEM((BLOCKS_PER_GROUP * n_pairs, rows, LANES), F32),
                        pltpu.VMEM((BLOCKS_PER_GROUP * n_pairs, rows, 1), F32)],
        compiler_params=pltpu.CompilerParams(
            dimension_semantics=("arbitrary", "arbitrary"),
            vmem_limit_bytes=VMEM_LIMIT_BYTES),
        name="stickbreak",
    )(q, k, v, attn_out_g.reshape(1, d_attn))


def _out_mlp_kernel(x_ref, pool_ref, attn_ref, w_out_ref, g2_ref, w_up_ref, w_down_ref, gf_ref, o_ref, *,
                    ff_chunk):
    mixed = jnp.concatenate([pool_ref[...], attn_ref[...]], axis=-1)
    h = x_ref[...] + jnp.dot(mixed, w_out_ref[...], preferred_element_type=F32)

    hn2 = _rms(h, g2_ref[...]).astype(BF16)
    d_ff = w_up_ref.shape[1]
    acc = h
    for c in range(d_ff // ff_chunk):
        up = jnp.dot(hn2, w_up_ref[:, c * ff_chunk:(c + 1) * ff_chunk], preferred_element_type=F32)
        act = jnp.square(jnp.maximum(up, 0.0)).astype(BF16)
        acc = acc + jnp.dot(act, w_down_ref[c * ff_chunk:(c + 1) * ff_chunk, :],
                            preferred_element_type=F32)
    o_ref[...] = _rms(acc, gf_ref[...])


def _out_mlp(x2, pool_n, attn_n, w_out, norm2_g, w_up, w_down, final_g, *, tm, ff_chunk):
    T, D = x2.shape
    d_pool = pool_n.shape[1]
    d_attn = attn_n.shape[1]
    d_ff = w_up.shape[1]
    kern = functools.partial(_out_mlp_kernel, ff_chunk=ff_chunk)
    tok = lambda i: (i, 0)
    const = lambda i: (0, 0)
    resident = functools.partial(pl.BlockSpec, index_map=const, pipeline_mode=pl.Buffered(1))
    return pl.pallas_call(
        kern,
        grid=(T // tm,),
        in_specs=[
            pl.BlockSpec((tm, D), tok),
            pl.BlockSpec((tm, d_pool), tok),
            pl.BlockSpec((tm, d_attn), tok),
            resident((d_pool + d_attn, D)),
            pl.BlockSpec((1, D), const),
            resident((D, d_ff)),
            resident((d_ff, D)),
            pl.BlockSpec((1, D), const),
        ],
        out_specs=pl.BlockSpec((tm, D), tok),
        out_shape=jax.ShapeDtypeStruct((T, D), F32),
        compiler_params=pltpu.CompilerParams(
            dimension_semantics=("arbitrary",),
            vmem_limit_bytes=VMEM_LIMIT_BYTES),
        name="out_mlp",
    )(x2, pool_n, attn_n, w_out, norm2_g.reshape(1, D), w_up, w_down, final_g.reshape(1, D))


def kernel(x, norm1_g, w_in, pool_w, pool_scale, pool_out_g, attn_out_g, w_out, norm2_g, w_up, w_down,
           final_g):
    B, S, D = x.shape
    assert S % PROJ_ROWS == 0 and S % (ATTN_BLOCK * ATTN_BLOCKS_PER_STEP) == 0 and (B * S) % MLP_ROWS == 0
    assert w_up.shape[1] % FF_CHUNK == 0
    x2 = x.reshape(B * S, D)
    pool_n, q, k, v, (w_out_bf, w_up_bf, w_down_bf) = _in_proj_pool(
        x2, S, norm1_g, w_in, pool_w, pool_scale, pool_out_g, [(w_out, 0), (w_up, 1), (w_down, 0)],
        tm=PROJ_ROWS)
    attn_n = _stickbreak(q, k, v, attn_out_g, t=ATTN_BLOCK, blocks_per_step=ATTN_BLOCKS_PER_STEP)
    out = _out_mlp(x2, pool_n, attn_n.reshape(B * S, -1), w_out_bf, norm2_g, w_up_bf, w_down_bf, final_g,
                   tm=MLP_ROWS, ff_chunk=FF_CHUNK)
    return out.reshape(B, S, D)
```

```python
import functools
import math

import jax
import jax.numpy as jnp
from jax import lax
from jax.experimental import pallas as pl
from jax.experimental.pallas import tpu as pltpu

EPS = 1e-6
LOG2_E = math.log2(math.e)
POOL_WINDOWS = (2, 4, 8, 16)
HEAD_DIM = 64
LANES = 128
MASKED_SCORE = -1e30
EXIT_LOG2 = 152.0
POOL_HALO = 16
POOL_PAD = 8
BF16_SUBLANES = 16
VMEM_LIMIT_BYTES = 56 * 1024 * 1024

MXU_DIM = 256
PROJ_ROWS = 2 * MXU_DIM
ATTN_BLOCK = MXU_DIM
BLOCKS_PER_GROUP = 4
ATTN_BLOCKS_PER_STEP = BLOCKS_PER_GROUP
MLP_ROWS = 4 * MXU_DIM
FF_CHUNK = 2 * MXU_DIM

BF16 = jnp.bfloat16
F32 = jnp.float32


def _rms(x, g):
    r = lax.rsqrt(jnp.mean(x * x, axis=-1, keepdims=True) + EPS)
    return x * r * g


def _in_proj_pool_kernel(x_ref, g1_ref, w_in_ref, pool_w_ref, pool_scale_ref, pool_g_ref, later_w_refs,
                         pool_out_ref, q_ref, k_ref, v_ref, later_w_bf_refs,
                         w_bf_ref, ubuf_ref, s2_ref, s4_ref, s8_ref, *,
                         tm, tiles_per_seq, n_tiles, d_pool, d_attn):
    g = pl.program_id(0)
    gdim = d_pool // len(POOL_WINDOWS)
    off = POOL_PAD + POOL_HALO
    levels = (ubuf_ref, s2_ref, s4_ref, s8_ref)

    @pl.when(g == 0)
    def _():
        w_bf_ref[...] = w_in_ref[...].astype(BF16)
        ubuf_ref[...] = jnp.zeros_like(ubuf_ref)
        for lvl in levels[1:]:
            lvl[0:POOL_PAD, :] = jnp.zeros((POOL_PAD, lvl.shape[1]), F32)

    for w_ref, w_bf in zip(later_w_refs, later_w_bf_refs):
        w_bf[...] = w_ref[...].astype(BF16)

    hn = _rms(x_ref[...], g1_ref[...]).astype(BF16)
    n_pairs = d_attn // LANES

    def project(first_col, width):
        return jnp.dot(hn, w_bf_ref[:, first_col:first_col + width], preferred_element_type=F32)

    def emit_heads(out_ref, first_col, scale=None):
        cols = project(first_col, d_attn)
        for p in range(n_pairs):
            blk = cols[:, p * LANES:(p + 1) * LANES]
            out_ref[0, p] = (blk if scale is None else blk * scale).astype(BF16)

    sums = [None] * len(POOL_WINDOWS)

    def window_level(n):
        shift = POOL_WINDOWS[n] // 2
        src = levels[n]
        if n + 1 < len(levels):
            doubled = src[POOL_PAD:off + tm, :] + src[POOL_PAD - shift:off + tm - shift, :]
            sums[n] = doubled[POOL_HALO:, 0:gdim]
            levels[n + 1][POOL_PAD:off + tm, :] = doubled[:, gdim:]
        else:
            sums[n] = src[off:off + tm, :] + src[off - shift:off + tm - shift, :]

    emit_heads(q_ref, d_pool, LOG2_E * HEAD_DIM ** -0.5)
    window_level(0)
    window_level(1)
    emit_heads(k_ref, d_pool + d_attn)
    window_level(2)
    window_level(3)
    emit_heads(v_ref, d_pool + 2 * d_attn)

    seq_tile = jnp.maximum(g - 1, 0) % tiles_per_seq
    head_pos = seq_tile * tm + lax.broadcasted_iota(jnp.int32, (POOL_HALO, 1), 0)
    mapped = []
    for grp, w in enumerate(POOL_WINDOWS):
        head_scale = w / jnp.minimum(head_pos + 1, w).astype(F32)
        mean = jnp.concatenate([sums[grp][0:POOL_HALO] * head_scale, sums[grp][POOL_HALO:]], axis=0)
        mean = mean * (1.0 / w)
        pooled = mean - ubuf_ref[off:off + tm, grp * gdim:(grp + 1) * gdim]
        mapped.append(jnp.dot(pooled.astype(BF16), pool_w_ref[grp].astype(BF16),
                              preferred_element_type=F32))
    y = jnp.concatenate(mapped, axis=-1) * pool_scale_ref[...]
    pool_out_ref[...] = _rms(y, pool_g_ref[...]).astype(BF16)

    starts_seq = jnp.minimum(g, n_tiles - 1) % tiles_per_seq == 0
    ubuf_ref[POOL_PAD:off, :] = jnp.where(starts_seq, 0.0, ubuf_ref[tm + POOL_PAD:tm + off, :])

    ubuf_ref[off:off + tm, :] = project(0, d_pool)


def _in_proj_pool(x2, seq_len, norm1_g, w_in, pool_w, pool_scale, pool_out_g, later_weights, *, tm):
    T, D = x2.shape
    n_groups, gdim, _ = pool_w.shape
    d_pool = n_groups * gdim
    d_attn = (w_in.shape[1] - d_pool) // 3
    tiles_per_seq = seq_len // tm
    n_tiles = T // tm
    n_pairs = d_attn // LANES
    kern = functools.partial(_in_proj_pool_kernel, tm=tm, tiles_per_seq=tiles_per_seq, n_tiles=n_tiles,
                             d_pool=d_pool, d_attn=d_attn)

    def proj_tile(g):
        return jnp.minimum(g, n_tiles - 1)

    def slab_spec(w, axis):
        shape = list(w.shape)
        shape[axis] //= n_tiles
        return pl.BlockSpec(tuple(shape), lambda g: tuple(proj_tile(g) if a == axis else 0 for a in range(2)))

    for w, axis in later_weights:
        assert w.ndim == 2 and w.shape[axis] % (n_tiles * BF16_SUBLANES) == 0, (w.shape, axis, n_tiles)
    slab_specs = [slab_spec(w, axis) for w, axis in later_weights]

    const2 = lambda g: (0, 0)
    qkv_spec = pl.BlockSpec((1, n_pairs, tm, LANES),
                            lambda g: (proj_tile(g) // tiles_per_seq, 0, proj_tile(g) % tiles_per_seq, 0))
    qkv_shape = jax.ShapeDtypeStruct((T // seq_len, n_pairs, seq_len, LANES), BF16)
    return pl.pallas_call(
        kern,
        grid=(n_tiles + 1,),
        in_specs=[
            pl.BlockSpec((tm, D), lambda g: (proj_tile(g), 0)),
            pl.BlockSpec((1, D), const2),
            pl.BlockSpec(w_in.shape, const2, pipeline_mode=pl.Buffered(1)),
            pl.BlockSpec(pool_w.shape, lambda g: (0, 0, 0)),
            pl.BlockSpec((1, d_pool), const2),
            pl.BlockSpec((1, d_pool), const2),
            slab_specs,
        ],
        out_specs=[pl.BlockSpec((tm, d_pool), lambda g: (jnp.maximum(g - 1, 0), 0)),
                   qkv_spec, qkv_spec, qkv_spec, slab_specs],
        out_shape=[jax.ShapeDtypeStruct((T, d_pool), BF16), qkv_shape, qkv_shape, qkv_shape,
                   [jax.ShapeDtypeStruct(w.shape, BF16) for w, _ in later_weights]],
        scratch_shapes=[pltpu.VMEM(w_in.shape, BF16)] +
                       [pltpu.VMEM((POOL_PAD + POOL_HALO + tm, d_pool - n * gdim), F32)
                        for n in range(n_groups)],
        compiler_params=pltpu.CompilerParams(
            dimension_semantics=("arbitrary",),
            vmem_limit_bytes=VMEM_LIMIT_BYTES),
        name="in_proj_pool",
    )(x2, norm1_g.reshape(1, D), w_in, pool_w, pool_scale.reshape(1, d_pool),
      pool_out_g.reshape(1, d_pool), [w for w, _ in later_weights])


def _stickbreak_kernel(q_ref, k_ref, v_ref, g_ref, o_ref, acc_ref, carry_ref, diag_ref, *, t, blocks_per_step):
    n_pairs = q_ref.shape[1]
    n_heads = LANES // HEAD_DIM
    rows = n_heads * t
    lane = lax.broadcasted_iota(jnp.int32, (t, LANES), 1)
    key_r = lax.broadcasted_iota(jnp.int32, (t, t), 0)
    key_c = lax.broadcasted_iota(jnp.int32, (t, t), 1)
    later_or_same = (key_r >= key_c).astype(BF16)
    qrow = lax.broadcasted_iota(jnp.int32, (rows, t), 0) & (t - 1)
    causal = lax.broadcasted_iota(jnp.int32, (rows, t), 1) < qrow
    half = t // 2
    late_old = (qrow >= half) & (lax.broadcasted_iota(jnp.int32, (rows, t), 1) < half)
    late_row = (lax.broadcasted_iota(jnp.int32, (rows, 1), 0) & (t - 1)) >= half

    def query_group(grp, _):
        first = grp * BLOCKS_PER_GROUP
        blk_i = [pl.program_id(1) * blocks_per_step + first + u for u in range(BLOCKS_PER_GROUP)]

        def q_rows(u):
            return pl.ds(pl.multiple_of((first + u) * t, t), t)

        def stacked_q(u, p):
            q2 = q_ref[0, p, q_rows(u), :]
            return jnp.concatenate(
                [jnp.where((lane >= h * HEAD_DIM) & (lane < (h + 1) * HEAD_DIM), q2, jnp.zeros_like(q2))
                 for h in range(n_heads)], axis=0)

        def pair_stages(p, qs, state, j0, j1, has_second, result):
            js = (j0, j1)
            zs, sps = [], []
            for n, j in enumerate(js):
                kb = k_ref[0, p, pl.ds(j * t, t), :]
                z = lax.dot_general(qs, kb, (((1,), (1,)), ((), ())), preferred_element_type=F32)
                if state is None and n == 0:
                    z = jnp.where(causal, z, MASKED_SCORE)
                if state is None and n == 1:
                    z = jnp.where(late_old, MASKED_SCORE, z)
                zs.append(z)
                sps.append(jnp.maximum(z, 0.0) + jnp.log2(1.0 + jnp.exp2(-jnp.abs(z))))
            yield
            sp_all = jnp.concatenate(sps, axis=0)
            from_here_all = jnp.dot(sp_all.astype(BF16), later_or_same, preferred_element_type=F32)
            yield
            pvs, block_sums = [], []
            for n, j in enumerate(js):
                from_here = from_here_all[n * rows:(n + 1) * rows]
                a = jnp.exp2(zs[n] - from_here)
                vb = v_ref[0, p, pl.ds(j * t, t), :]
                pvs.append(jnp.dot(a.astype(BF16), vb, preferred_element_type=F32))
                block_sums.append(from_here[:, 0:1])
            if state is None:
                carry, acc = block_sums[0], pvs[0]
            else:
                carry, acc = state
                acc = acc + jnp.exp2(-carry) * pvs[0]
                carry = carry + block_sums[0]
            acc = acc + jnp.where(has_second, jnp.exp2(-carry), 0.0) * pvs[1]
            carry = carry + jnp.where(has_second, block_sums[1], 0.0)
            result.append((carry, acc, block_sums[0]))

        chains = [(u, p) for u in range(BLOCKS_PER_GROUP) for p in range(n_pairs)]
        results = [[] for _ in chains]
        stages = [pair_stages(p, stacked_q(u, p), None, blk_i[u], jnp.maximum(blk_i[u] - 1, 0),
                              blk_i[u] >= 1, results[c]) for c, (u, p) in enumerate(chains)]
        n_stages = 3
        for tick in range(len(chains) + n_stages - 1):
            for c in range(len(chains)):
                if 0 <= tick - c < n_stages:
                    next(stages[c], None)

        needs_more = None
        needs_old_half = None
        for c, (u, p) in enumerate(chains):
            carry, acc, diag_sum = results[c][0]
            acc_ref[c] = acc
            carry_ref[c] = carry
            diag_ref[c] = diag_sum
            more = (blk_i[u] >= 2) & (jnp.min(carry) < EXIT_LOG2)
            needs_more = more if needs_more is None else needs_more | more
            late_min = jnp.min(jnp.where(late_row, carry, jnp.inf))
            old_half = (blk_i[u] >= 1) & (late_min < EXIT_LOG2)
            needs_old_half = old_half if needs_old_half is None else needs_old_half | old_half

        @pl.when(needs_old_half)
        def _():
            def per_chain(c, _):
                u = c // n_pairs
                p = c % n_pairs
                i = blk_i[0] + u
                valid = i >= 1
                j = jnp.maximum(i - 1, 0)
                qs = stacked_q(u, p)
                kb = k_ref[0, p, pl.ds(j * t, half), :]
                vb = v_ref[0, p, pl.ds(j * t, half), :]
                z = lax.dot_general(qs, kb, (((1,), (1,)), ((), ())), preferred_element_type=F32)
                sp = jnp.maximum(z, 0.0) + jnp.log2(1.0 + jnp.exp2(-jnp.abs(z)))
                sp = jnp.where(late_row, sp, 0.0)
                within = jnp.dot(sp.astype(BF16), later_or_same[0:half, 0:half], preferred_element_type=F32)
                carry = carry_ref[c]
                diag_sum = diag_ref[c]
                a = jnp.where(late_row, jnp.exp2(z - within - (carry - diag_sum)), 0.0)
                pv = jnp.dot(a.astype(BF16), vb, preferred_element_type=F32)
                acc_ref[c] = acc_ref[c] + jnp.where(valid, jnp.exp2(-diag_sum), 0.0) * pv
                carry_ref[c] = carry + jnp.where(valid & late_row, within[:, 0:1], 0.0)
                return 0

            lax.fori_loop(0, len(chains), per_chain, 0)

        @pl.when(needs_more)
        def _():
            def per_chain(c, _):
                u = c // n_pairs
                p = c % n_pairs
                i = blk_i[0] + u
                qs = stacked_q(u, p)

                def more_blocks(state):
                    n, chain_min, _, _ = state
                    return (n < i // 2) & (chain_min < EXIT_LOG2)

                def pair_body(state):
                    n, _, carry, acc = state
                    j0 = i - 2 - 2 * n
                    result = []
                    for _ in pair_stages(p, qs, (carry, acc), j0, jnp.maximum(j0 - 1, 0), j0 >= 1, result):
                        pass
                    carry, acc, _ = result[0]
                    return n + 1, jnp.min(carry), carry, acc

                carry = carry_ref[c]
                _, _, _, acc = lax.while_loop(more_blocks, pair_body,
                                              (0, jnp.min(carry), carry, acc_ref[c]))
                acc_ref[c] = acc
                return 0

            lax.fori_loop(0, len(chains), per_chain, 0)

        for u in range(BLOCKS_PER_GROUP):
            y = jnp.concatenate(
                [jnp.where(lane < HEAD_DIM, acc_ref[u * n_pairs + p, 0:t], acc_ref[u * n_pairs + p, t:2 * t])
                 for p in range(n_pairs)], axis=-1)
            o_ref[0, q_rows(u), :] = _rms(y, g_ref[...]).astype(BF16)
        return 0

    lax.fori_loop(0, blocks_per_step // BLOCKS_PER_GROUP, query_group, 0)


def _stickbreak(q, k, v, attn_out_g, *, t, blocks_per_step):
    B, n_pairs, S, _ = q.shape
    d_attn = n_pairs * LANES
    rows = (LANES // HEAD_DIM) * t
    ts = t * blocks_per_step
    kern = functools.partial(_stickbreak_kernel, t=t, blocks_per_step=blocks_per_step)
    full = lambda b, i: (b, 0, 0, 0)
    return pl.pallas_call(
        kern,
        grid=(B, S // ts),
        in_specs=[
            pl.BlockSpec((1, n_pairs, ts, LANES), lambda b, i: (b, 0, i, 0)),
            pl.BlockSpec((1, n_pairs, S, LANES), full),
            pl.BlockSpec((1, n_pairs, S, LANES), full),
            pl.BlockSpec((1, d_attn), lambda b, i: (0, 0)),
        ],
        out_specs=pl.BlockSpec((1, ts, d_attn), lambda b, i: (b, i, 0)),
        out_shape=jax.ShapeDtypeStruct((B, S, d_attn), BF16),
        scratch_shapes=[pltpu.VMEM((BLOCKS_PER_GROUP * n_pairs, rows, LANES), F32),
                        pltpu.VMEM((BLOCKS_PER_GROUP * n_pairs, rows, 1), F32),
                        pltpu.VMEM((BLOCKS_PER_GROUP * n_pairs, rows, 1), F32)],
        compiler_params=pltpu.CompilerParams(
            dimension_semantics=("arbitrary", "arbitrary"),
            vmem_limit_bytes=VMEM_LIMIT_BYTES),
        name="stickbreak",
    )(q, k, v, attn_out_g.reshape(1, d_attn))


def _out_mlp_kernel(x_ref, pool_ref, attn_ref, w_out_ref, g2_ref, w_up_ref, w_down_ref, gf_ref, o_ref, *,
                    ff_chunk):
    mixed = jnp.concatenate([pool_ref[...], attn_ref[...]], axis=-1)
    h = x_ref[...] + jnp.dot(mixed, w_out_ref[...], preferred_element_type=F32)

    hn2 = _rms(h, g2_ref[...]).astype(BF16)
    d_ff = w_up_ref.shape[1]
    acc = h
    for c in range(d_ff // ff_chunk):
        up = jnp.dot(hn2, w_up_ref[:, c * ff_chunk:(c + 1) * ff_chunk], preferred_element_type=F32)
        act = jnp.square(jnp.maximum(up, 0.0)).astype(BF16)
        acc = acc + jnp.dot(act, w_down_ref[c * ff_chunk:(c + 1) * ff_chunk, :],
                            preferred_element_type=F32)
    o_ref[...] = _rms(acc, gf_ref[...])


def _out_mlp(x2, pool_n, attn_n, w_out, norm2_g, w_up, w_down, final_g, *, tm, ff_chunk):
    T, D = x2.shape
    d_pool = pool_n.shape[1]
    d_attn = attn_n.shape[1]
    d_ff = w_up.shape[1]
    kern = functools.partial(_out_mlp_kernel, ff_chunk=ff_chunk)
    tok = lambda i: (i, 0)
    const = lambda i: (0, 0)
    resident = functools.partial(pl.BlockSpec, index_map=const, pipeline_mode=pl.Buffered(1))
    return pl.pallas_call(
        kern,
        grid=(T // tm,),
        in_specs=[
            pl.BlockSpec((tm, D), tok),
            pl.BlockSpec((tm, d_pool), tok),
            pl.BlockSpec((tm, d_attn), tok),
            resident((d_pool + d_attn, D)),
            pl.BlockSpec((1, D), const),
            resident((D, d_ff)),
            resident((d_ff, D)),
            pl.BlockSpec((1, D), const),
        ],
        out_specs=pl.BlockSpec((tm, D), tok),
        out_shape=jax.ShapeDtypeStruct((T, D), F32),
        compiler_params=pltpu.CompilerParams(
            dimension_semantics=("arbitrary",),
            vmem_limit_bytes=VMEM_LIMIT_BYTES),
        name="out_mlp",
    )(x2, pool_n, attn_n, w_out, norm2_g.reshape(1, D), w_up, w_down, final_g.reshape(1, D))


def kernel(x, norm1_g, w_in, pool_w, pool_scale, pool_out_g, attn_out_g, w_out, norm2_g, w_up, w_down,
           final_g):
    B, S, D = x.shape
    assert S % PROJ_ROWS == 0 and S % (ATTN_BLOCK * ATTN_BLOCKS_PER_STEP) == 0 and (B * S) % MLP_ROWS == 0
    assert w_up.shape[1] % FF_CHUNK == 0
    x2 = x.reshape(B * S, D)
    pool_n, q, k, v, (w_out_bf, w_up_bf, w_down_bf) = _in_proj_pool(
        x2, S, norm1_g, w_in, pool_w, pool_scale, pool_out_g, [(w_out, 0), (w_up, 1), (w_down, 0)],
        tm=PROJ_ROWS)
    attn_n = _stickbreak(q, k, v, attn_out_g, t=ATTN_BLOCK, blocks_per_step=ATTN_BLOCKS_PER_STEP)
    out = _out_mlp(x2, pool_n, attn_n.reshape(B * S, -1), w_out_bf, norm2_g, w_up_bf, w_down_bf, final_g,
                   tm=MLP_ROWS, ff_chunk=FF_CHUNK)
    return out.reshape(B, S, D)
```

```python
import functools
import math

import jax
import jax.numpy as jnp
from jax import lax
from jax.experimental import pallas as pl
from jax.experimental.pallas import tpu as pltpu

EPS = 1e-6
LOG2_E = math.log2(math.e)
POOL_WINDOWS = (2, 4, 8, 16)
HEAD_DIM = 64
LANES = 128
MASKED_SCORE = -1e30
EXIT_LOG2 = 152.0
POOL_HALO = 16
POOL_PAD = 8
BF16_SUBLANES = 16
VMEM_LIMIT_BYTES = 56 * 1024 * 1024

MXU_DIM = 256
PROJ_ROWS = 2 * MXU_DIM
ATTN_BLOCK = MXU_DIM
BLOCKS_PER_GROUP = 4
ATTN_BLOCKS_PER_STEP = BLOCKS_PER_GROUP
MLP_ROWS = 4 * MXU_DIM
FF_CHUNK = 2 * MXU_DIM

BF16 = jnp.bfloat16
F32 = jnp.float32


def _rms(x, g):
    r = lax.rsqrt(jnp.mean(x * x, axis=-1, keepdims=True) + EPS)
    return x * r * g


def _in_proj_pool_kernel(x_ref, g1_ref, w_in_ref, pool_w_ref, pool_scale_ref, pool_g_ref, later_w_refs,
                         pool_out_ref, q_ref, k_ref, v_ref, later_w_bf_refs,
                         w_bf_ref, ubuf_ref, s2_ref, s4_ref, s8_ref, *,
                         tm, tiles_per_seq, n_tiles, d_pool, d_attn):
    g = pl.program_id(0)
    gdim = d_pool // len(POOL_WINDOWS)
    off = POOL_PAD + POOL_HALO
    levels = (ubuf_ref, s2_ref, s4_ref, s8_ref)

    @pl.when(g == 0)
    def _():
        w_bf_ref[...] = w_in_ref[...].astype(BF16)
        ubuf_ref[...] = jnp.zeros_like(ubuf_ref)
        for lvl in levels[1:]:
            lvl[0:POOL_PAD, :] = jnp.zeros((POOL_PAD, lvl.shape[1]), F32)

    for w_ref, w_bf in zip(later_w_refs, later_w_bf_refs):
        w_bf[...] = w_ref[...].astype(BF16)

    hn = _rms(x_ref[...], g1_ref[...]).astype(BF16)
    n_pairs = d_attn // LANES

    def project(first_col, width):
        return jnp.dot(hn, w_bf_ref[:, first_col:first_col + width], preferred_element_type=F32)

    def emit_heads(out_ref, first_col, scale=None):
        cols = project(first_col, d_attn)
        for p in range(n_pairs):
            blk = cols[:, p * LANES:(p + 1) * LANES]
            out_ref[0, p] = (blk if scale is None else blk * scale).astype(BF16)

    sums = [None] * len(POOL_WINDOWS)

    def window_level(n):
        shift = POOL_WINDOWS[n] // 2
        src = levels[n]
        if n + 1 < len(levels):
            doubled = src[POOL_PAD:off + tm, :] + src[POOL_PAD - shift:off + tm - shift, :]
            sums[n] = doubled[POOL_HALO:, 0:gdim]
            levels[n + 1][POOL_PAD:off + tm, :] = doubled[:, gdim:]
        else:
            sums[n] = src[off:off + tm, :] + src[off - shift:off + tm - shift, :]

    emit_heads(q_ref, d_pool, LOG2_E * HEAD_DIM ** -0.5)
    window_level(0)
    window_level(1)
    emit_heads(k_ref, d_pool + d_attn)
    window_level(2)
    window_level(3)
    emit_heads(v_ref, d_pool + 2 * d_attn)

    seq_tile = jnp.maximum(g - 1, 0) % tiles_per_seq
    head_pos = seq_tile * tm + lax.broadcasted_iota(jnp.int32, (POOL_HALO, 1), 0)
    mapped = []
    for grp, w in enumerate(POOL_WINDOWS):
        head_scale = w / jnp.minimum(head_pos + 1, w).astype(F32)
        mean = jnp.concatenate([sums[grp][0:POOL_HALO] * head_scale, sums[grp][POOL_HALO:]], axis=0)
        mean = mean * (1.0 / w)
        pooled = mean - ubuf_ref[off:off + tm, grp * gdim:(grp + 1) * gdim]
        mapped.append(jnp.dot(pooled.astype(BF16), pool_w_ref[grp].astype(BF16),
                              preferred_element_type=F32))
    y = jnp.concatenate(mapped, axis=-1) * pool_scale_ref[...]
    pool_out_ref[...] = _rms(y, pool_g_ref[...]).astype(BF16)

    starts_seq = jnp.minimum(g, n_tiles - 1) % tiles_per_seq == 0
    ubuf_ref[POOL_PAD:off, :] = jnp.where(starts_seq, 0.0, ubuf_ref[tm + POOL_PAD:tm + off, :])

    ubuf_ref[off:off + tm, :] = project(0, d_pool)


def _in_proj_pool(x2, seq_len, norm1_g, w_in, pool_w, pool_scale, pool_out_g, later_weights, *, tm):
    T, D = x2.shape
    n_groups, gdim, _ = pool_w.shape
    d_pool = n_groups * gdim
    d_attn = (w_in.shape[1] - d_pool) // 3
    tiles_per_seq = seq_len // tm
    n_tiles = T // tm
    n_pairs = d_attn // LANES
    kern = functools.partial(_in_proj_pool_kernel, tm=tm, tiles_per_seq=tiles_per_seq, n_tiles=n_tiles,
                             d_pool=d_pool, d_attn=d_attn)

    def proj_tile(g):
        return jnp.minimum(g, n_tiles - 1)

    def slab_spec(w, axis):
        shape = list(w.shape)
        shape[axis] //= n_tiles
        return pl.BlockSpec(tuple(shape), lambda g: tuple(proj_tile(g) if a == axis else 0 for a in range(2)))

    for w, axis in later_weights:
        assert w.ndim == 2 and w.shape[axis] % (n_tiles * BF16_SUBLANES) == 0, (w.shape, axis, n_tiles)
    slab_specs = [slab_spec(w, axis) for w, axis in later_weights]

    const2 = lambda g: (0, 0)
    qkv_spec = pl.BlockSpec((1, n_pairs, tm, LANES),
                            lambda g: (proj_tile(g) // tiles_per_seq, 0, proj_tile(g) % tiles_per_seq, 0))
    qkv_shape = jax.ShapeDtypeStruct((T // seq_len, n_pairs, seq_len, LANES), BF16)
    return pl.pallas_call(
        kern,
        grid=(n_tiles + 1,),
        in_specs=[
            pl.BlockSpec((tm, D), lambda g: (proj_tile(g), 0)),
            pl.BlockSpec((1, D), const2),
            pl.BlockSpec(w_in.shape, const2, pipeline_mode=pl.Buffered(1)),
            pl.BlockSpec(pool_w.shape, lambda g: (0, 0, 0)),
            pl.BlockSpec((1, d_pool), const2),
            pl.BlockSpec((1, d_pool), const2),
            slab_specs,
        ],
        out_specs=[pl.BlockSpec((tm, d_pool), lambda g: (jnp.maximum(g - 1, 0), 0)),
                   qkv_spec, qkv_spec, qkv_spec, slab_specs],
        out_shape=[jax.ShapeDtypeStruct((T, d_pool), BF16), qkv_shape, qkv_shape, qkv_shape,
                   [jax.ShapeDtypeStruct(w.shape, BF16) for w, _ in later_weights]],
        scratch_shapes=[pltpu.VMEM(w_in.shape, BF16)] +
                       [pltpu.VMEM((POOL_PAD + POOL_HALO + tm, d_pool - n * gdim), F32)
                        for n in range(n_groups)],
        compiler_params=pltpu.CompilerParams(
            dimension_semantics=("arbitrary",),
            vmem_limit_bytes=VMEM_LIMIT_BYTES),
        name="in_proj_pool",
    )(x2, norm1_g.reshape(1, D), w_in, pool_w, pool_scale.reshape(1, d_pool),
      pool_out_g.reshape(1, d_pool), [w for w, _ in later_weights])


def _stickbreak_kernel(q_ref, k_ref, v_ref, g_ref, o_ref, acc_ref, carry_ref, diag_ref, *, t, blocks_per_step):
    n_pairs = q_ref.shape[1]
    n_heads = LANES // HEAD_DIM
    rows = n_heads * t
    lane = lax.broadcasted_iota(jnp.int32, (t, LANES), 1)
    key_r = lax.broadcasted_iota(jnp.int32, (t, t), 0)
    key_c = lax.broadcasted_iota(jnp.int32, (t, t), 1)
    later_or_same = (key_r >= key_c).astype(BF16)
    qrow = lax.broadcasted_iota(jnp.int32, (rows, t), 0) & (t - 1)
    causal = lax.broadcasted_iota(jnp.int32, (rows, t), 1) < qrow
    half = t // 2
    late_old = (qrow >= half) & (lax.broadcasted_iota(jnp.int32, (rows, t), 1) < half)
    late_row = (lax.broadcasted_iota(jnp.int32, (rows, 1), 0) & (t - 1)) >= half

    def query_group(grp, _):
        first = grp * BLOCKS_PER_GROUP
        blk_i = [pl.program_id(1) * blocks_per_step + first + u for u in range(BLOCKS_PER_GROUP)]

        def q_rows(u):
            return pl.ds(pl.multiple_of((first + u) * t, t), t)

        def stacked_q(u, p):
            q2 = q_ref[0, p, q_rows(u), :]
            return jnp.concatenate(
                [jnp.where((lane >= h * HEAD_DIM) & (lane < (h + 1) * HEAD_DIM), q2, jnp.zeros_like(q2))
                 for h in range(n_heads)], axis=0)

        def pair_stages(p, qs, state, j0, j1, has_second, result):
            js = (j0, j1)
            zs, sps = [], []
            for n, j in enumerate(js):
                kb = k_ref[0, p, pl.ds(j * t, t), :]
                z = lax.dot_general(qs, kb, (((1,), (1,)), ((), ())), preferred_element_type=F32)
                if state is None and n == 0:
                    z = jnp.where(causal, z, MASKED_SCORE)
                if state is None and n == 1:
                    z = jnp.where(late_old, MASKED_SCORE, z)
                zs.append(z)
                sps.append(jnp.maximum(z, 0.0) + jnp.log2(1.0 + jnp.exp2(-jnp.abs(z))))
            yield
            sp_all = jnp.concatenate(sps, axis=0)
            from_here_all = jnp.dot(sp_all.astype(BF16), later_or_same, preferred_element_type=F32)
            yield
            pvs, block_sums = [], []
            for n, j in enumerate(js):
                from_here = from_here_all[n * rows:(n + 1) * rows]
                a = jnp.exp2(zs[n] - from_here)
                vb = v_ref[0, p, pl.ds(j * t, t), :]
                pvs.append(jnp.dot(a.astype(BF16), vb, preferred_element_type=F32))
                block_sums.append(from_here[:, 0:1])
            if state is None:
                carry, acc = block_sums[0], pvs[0]
            else:
                carry, acc = state
                acc = acc + jnp.exp2(-carry) * pvs[0]
                carry = carry + block_sums[0]
            if has_second is True:
                acc = acc + jnp.exp2(-carry) * pvs[1]
                carry = carry + block_sums[1]
            else:
                acc = acc + jnp.where(has_second, jnp.exp2(-carry), 0.0) * pvs[1]
                carry = carry + jnp.where(has_second, block_sums[1], 0.0)
            result.append((carry, acc, block_sums[0]))

        chains = [(u, p) for u in range(BLOCKS_PER_GROUP) for p in range(n_pairs)]
        results = [[] for _ in chains]
        stages = [pair_stages(p, stacked_q(u, p), None, blk_i[u], jnp.maximum(blk_i[u] - 1, 0),
                              True if u >= 1 else blk_i[u] >= 1, results[c])
                  for c, (u, p) in enumerate(chains)]
        n_stages = 3
        for tick in range(len(chains) + n_stages - 1):
            for c in reversed(range(len(chains))):
                if 0 <= tick - c < n_stages:
                    next(stages[c], None)

        needs_more = None
        needs_old_half = None
        for c, (u, p) in enumerate(chains):
            carry, acc, diag_sum = results[c][0]
            acc_ref[c] = acc
            carry_ref[c] = carry
            diag_ref[c] = diag_sum
            more = (blk_i[u] >= 2) & (jnp.min(carry) < EXIT_LOG2)
            needs_more = more if needs_more is None else needs_more | more
            late_min = jnp.min(jnp.where(late_row, carry, jnp.inf))
            old_half = (blk_i[u] >= 1) & (late_min < EXIT_LOG2)
            needs_old_half = old_half if needs_old_half is None else needs_old_half | old_half

        @pl.when(needs_old_half)
        def _():
            def per_chain(c, _):
                u = c // n_pairs
                p = c % n_pairs
                i = blk_i[0] + u
                valid = i >= 1
                j = jnp.maximum(i - 1, 0)
                qs = stacked_q(u, p)
                kb = k_ref[0, p, pl.ds(j * t, half), :]
                vb = v_ref[0, p, pl.ds(j * t, half), :]
                z = lax.dot_general(qs, kb, (((1,), (1,)), ((), ())), preferred_element_type=F32)
                sp = jnp.maximum(z, 0.0) + jnp.log2(1.0 + jnp.exp2(-jnp.abs(z)))
                sp = jnp.where(late_row, sp, 0.0)
                within = jnp.dot(sp.astype(BF16), later_or_same[0:half, 0:half], preferred_element_type=F32)
                carry = carry_ref[c]
                diag_sum = diag_ref[c]
                a = jnp.where(late_row, jnp.exp2(z - within - (carry - diag_sum)), 0.0)
                pv = jnp.dot(a.astype(BF16), vb, preferred_element_type=F32)
                acc_ref[c] = acc_ref[c] + jnp.where(valid, jnp.exp2(-diag_sum), 0.0) * pv
                carry_ref[c] = carry + jnp.where(valid & late_row, within[:, 0:1], 0.0)
                return 0

            lax.fori_loop(0, len(chains), per_chain, 0)

        @pl.when(needs_more)
        def _():
            def per_chain(c, _):
                u = c // n_pairs
                p = c % n_pairs
                i = blk_i[0] + u
                qs = stacked_q(u, p)

                def more_blocks(state):
                    n, chain_min, _, _ = state
                    return (n < i // 2) & (chain_min < EXIT_LOG2)

                def pair_body(state):
                    n, _, carry, acc = state
                    j0 = i - 2 - 2 * n
                    result = []
                    for _ in pair_stages(p, qs, (carry, acc), j0, jnp.maximum(j0 - 1, 0), j0 >= 1, result):
                        pass
                    carry, acc, _ = result[0]
                    return n + 1, jnp.min(carry), carry, acc

                carry = carry_ref[c]
                _, _, _, acc = lax.while_loop(more_blocks, pair_body,
                                              (0, jnp.min(carry), carry, acc_ref[c]))
                acc_ref[c] = acc
                return 0

            lax.fori_loop(0, len(chains), per_chain, 0)

        for u in range(BLOCKS_PER_GROUP):
            y = jnp.concatenate(
                [jnp.where(lane < HEAD_DIM, acc_ref[u * n_pairs + p, 0:t], acc_ref[u * n_pairs + p, t:2 * t])
                 for p in range(n_pairs)], axis=-1)
            o_ref[0, q_rows(u), :] = _rms(y, g_ref[...]).astype(BF16)
        return 0

    lax.fori_loop(0, blocks_per_step // BLOCKS_PER_GROUP, query_group, 0)


def _stickbreak(q, k, v, attn_out_g, *, t, blocks_per_step):
    B, n_pairs, S, _ = q.shape
    d_attn = n_pairs * LANES
    rows = (LANES // HEAD_DIM) * t
    ts = t * blocks_per_step
    kern = functools.partial(_stickbreak_kernel, t=t, blocks_per_step=blocks_per_step)
    full = lambda b, i: (b, 0, 0, 0)
    return pl.pallas_call(
        kern,
        grid=(B, S // ts),
        in_specs=[
            pl.BlockSpec((1, n_pairs, ts, LANES), lambda b, i: (b, 0, i, 0)),
            pl.BlockSpec((1, n_pairs, S, LANES), full),
            pl.BlockSpec((1, n_pairs, S, LANES), full),
            pl.BlockSpec((1, d_attn), lambda b, i: (0, 0)),
        ],
        out_specs=pl.BlockSpec((1, ts, d_attn), lambda b, i: (b, i, 0)),
        out_shape=jax.ShapeDtypeStruct((B, S, d_attn), BF16),
        scratch_shapes=[pltpu.VMEM((BLOCKS_PER_GROUP * n_pairs, rows, LANES), F32),
                        pltpu.VMEM((BLOCKS_PER_GROUP * n_pairs, rows, 1), F32),
                        pltpu.VMEM((BLOCKS_PER_GROUP * n_pairs, rows, 1), F32)],
        compiler_params=pltpu.CompilerParams(
            dimension_semantics=("arbitrary", "arbitrary"),
            vmem_limit_bytes=VMEM_LIMIT_BYTES),
        name="stickbreak",
    )(q, k, v, attn_out_g.reshape(1, d_attn))


def _out_mlp_kernel(x_ref, pool_ref, attn_ref, w_out_ref, g2_ref, w_up_ref, w_down_ref, gf_ref, o_ref, *,
                    ff_chunk):
    mixed = jnp.concatenate([pool_ref[...], attn_ref[...]], axis=-1)
    h = x_ref[...] + jnp.dot(mixed, w_out_ref[...], preferred_element_type=F32)

    hn2 = _rms(h, g2_ref[...]).astype(BF16)
    d_ff = w_up_ref.shape[1]
    acc = h
    for c in range(d_ff // ff_chunk):
        up = jnp.dot(hn2, w_up_ref[:, c * ff_chunk:(c + 1) * ff_chunk], preferred_element_type=F32)
        act = jnp.square(jnp.maximum(up, 0.0)).astype(BF16)
        acc = acc + jnp.dot(act, w_down_ref[c * ff_chunk:(c + 1) * ff_chunk, :],
                            preferred_element_type=F32)
    o_ref[...] = _rms(acc, gf_ref[...])


def _out_mlp(x2, pool_n, attn_n, w_out, norm2_g, w_up, w_down, final_g, *, tm, ff_chunk):
    T, D = x2.shape
    d_pool = pool_n.shape[1]
    d_attn = attn_n.shape[1]
    d_ff = w_up.shape[1]
    kern = functools.partial(_out_mlp_kernel, ff_chunk=ff_chunk)
    tok = lambda i: (i, 0)
    const = lambda i: (0, 0)
    resident = functools.partial(pl.BlockSpec, index_map=const, pipeline_mode=pl.Buffered(1))
    return pl.pallas_call(
        kern,
        grid=(T // tm,),
        in_specs=[
            pl.BlockSpec((tm, D), tok),
            pl.BlockSpec((tm, d_pool), tok),
            pl.BlockSpec((tm, d_attn), tok),
            resident((d_pool + d_attn, D)),
            pl.BlockSpec((1, D), const),
            resident((D, d_ff)),
            resident((d_ff, D)),
            pl.BlockSpec((1, D), const),
        ],
        out_specs=pl.BlockSpec((tm, D), tok),
        out_shape=jax.ShapeDtypeStruct((T, D), F32),
        compiler_params=pltpu.CompilerParams(
            dimension_semantics=("arbitrary",),
            vmem_limit_bytes=VMEM_LIMIT_BYTES),
        name="out_mlp",
    )(x2, pool_n, attn_n, w_out, norm2_g.reshape(1, D), w_up, w_down, final_g.reshape(1, D))


def kernel(x, norm1_g, w_in, pool_w, pool_scale, pool_out_g, attn_out_g, w_out, norm2_g, w_up, w_down,
           final_g):
    B, S, D = x.shape
    assert S % PROJ_ROWS == 0 and S % (ATTN_BLOCK * ATTN_BLOCKS_PER_STEP) == 0 and (B * S) % MLP_ROWS == 0
    assert w_up.shape[1] % FF_CHUNK == 0
    x2 = x.reshape(B * S, D)
    pool_n, q, k, v, (w_out_bf, w_up_bf, w_down_bf) = _in_proj_pool(
        x2, S, norm1_g, w_in, pool_w, pool_scale, pool_out_g, [(w_out, 0), (w_up, 1), (w_down, 0)],
        tm=PROJ_ROWS)
    attn_n = _stickbreak(q, k, v, attn_out_g, t=ATTN_BLOCK, blocks_per_step=ATTN_BLOCKS_PER_STEP)
    out = _out_mlp(x2, pool_n, attn_n.reshape(B * S, -1), w_out_bf, norm2_g, w_up_bf, w_down_bf, final_g,
                   tm=MLP_ROWS, ff_chunk=FF_CHUNK)
    return out.reshape(B, S, D)
```

```python
import functools
import math

import jax
import jax.numpy as jnp
from jax import lax
from jax.experimental import pallas as pl
from jax.experimental.pallas import tpu as pltpu

EPS = 1e-6
LOG2_E = math.log2(math.e)
POOL_WINDOWS = (2, 4, 8, 16)
HEAD_DIM = 64
LANES = 128
MASKED_SCORE = -1e30
EXIT_LOG2 = 152.0
POOL_HALO = 16
POOL_PAD = 8
BF16_SUBLANES = 16
VMEM_LIMIT_BYTES = 56 * 1024 * 1024

MXU_DIM = 256
PROJ_ROWS = 2 * MXU_DIM
ATTN_BLOCK = MXU_DIM
BLOCKS_PER_GROUP = 4
ATTN_BLOCKS_PER_STEP = BLOCKS_PER_GROUP
MLP_ROWS = 4 * MXU_DIM
FF_CHUNK = 2 * MXU_DIM

BF16 = jnp.bfloat16
F32 = jnp.float32


def _rms(x, g):
    r = lax.rsqrt(jnp.mean(x * x, axis=-1, keepdims=True) + EPS)
    return x * r * g


def _in_proj_pool_kernel(x_ref, g1_ref, w_in_ref, pool_w_ref, pool_scale_ref, pool_g_ref, later_w_refs,
                         pool_out_ref, q_ref, k_ref, v_ref, later_w_bf_refs,
                         w_bf_ref, ubuf_ref, s2_ref, s4_ref, s8_ref, *,
                         tm, tiles_per_seq, n_tiles, d_pool, d_attn):
    g = pl.program_id(0)
    gdim = d_pool // len(POOL_WINDOWS)
    off = POOL_PAD + POOL_HALO
    levels = (ubuf_ref, s2_ref, s4_ref, s8_ref)

    @pl.when(g == 0)
    def _():
        w_bf_ref[...] = w_in_ref[...].astype(BF16)
        ubuf_ref[...] = jnp.zeros_like(ubuf_ref)
        for lvl in levels[1:]:
            lvl[0:POOL_PAD, :] = jnp.zeros((POOL_PAD, lvl.shape[1]), F32)

    for w_ref, w_bf in zip(later_w_refs, later_w_bf_refs):
        w_bf[...] = w_ref[...].astype(BF16)

    hn = _rms(x_ref[...], g1_ref[...]).astype(BF16)
    n_pairs = d_attn // LANES

    def project(first_col, width):
        return jnp.dot(hn, w_bf_ref[:, first_col:first_col + width], preferred_element_type=F32)

    def emit_heads(out_ref, first_col, scale=None):
        cols = project(first_col, d_attn)
        for p in range(n_pairs):
            blk = cols[:, p * LANES:(p + 1) * LANES]
            out_ref[0, p] = (blk if scale is None else blk * scale).astype(BF16)

    sums = [None] * len(POOL_WINDOWS)

    def window_level(n):
        shift = POOL_WINDOWS[n] // 2
        src = levels[n]
        if n + 1 < len(levels):
            doubled = src[POOL_PAD:off + tm, :] + src[POOL_PAD - shift:off + tm - shift, :]
            sums[n] = doubled[POOL_HALO:, 0:gdim]
            levels[n + 1][POOL_PAD:off + tm, :] = doubled[:, gdim:]
        else:
            sums[n] = src[off:off + tm, :] + src[off - shift:off + tm - shift, :]

    emit_heads(q_ref, d_pool, LOG2_E * HEAD_DIM ** -0.5)
    window_level(0)
    window_level(1)
    emit_heads(k_ref, d_pool + d_attn)
    window_level(2)
    window_level(3)
    emit_heads(v_ref, d_pool + 2 * d_attn)

    seq_tile = jnp.maximum(g - 1, 0) % tiles_per_seq
    head_pos = seq_tile * tm + lax.broadcasted_iota(jnp.int32, (POOL_HALO, 1), 0)
    mapped = []
    for grp, w in enumerate(POOL_WINDOWS):
        head_scale = w / jnp.minimum(head_pos + 1, w).astype(F32)
        mean = jnp.concatenate([sums[grp][0:POOL_HALO] * head_scale, sums[grp][POOL_HALO:]], axis=0)
        mean = mean * (1.0 / w)
        pooled = mean - ubuf_ref[off:off + tm, grp * gdim:(grp + 1) * gdim]
        mapped.append(jnp.dot(pooled.astype(BF16), pool_w_ref[grp].astype(BF16),
                              preferred_element_type=F32))
    y = jnp.concatenate(mapped, axis=-1) * pool_scale_ref[...]
    pool_out_ref[...] = _rms(y, pool_g_ref[...]).astype(BF16)

    starts_seq = jnp.minimum(g, n_tiles - 1) % tiles_per_seq == 0
    ubuf_ref[POOL_PAD:off, :] = jnp.where(starts_seq, 0.0, ubuf_ref[tm + POOL_PAD:tm + off, :])

    ubuf_ref[off:off + tm, :] = project(0, d_pool)


def _in_proj_pool(x2, seq_len, norm1_g, w_in, pool_w, pool_scale, pool_out_g, later_weights, *, tm):
    T, D = x2.shape
    n_groups, gdim, _ = pool_w.shape
    d_pool = n_groups * gdim
    d_attn = (w_in.shape[1] - d_pool) // 3
    tiles_per_seq = seq_len // tm
    n_tiles = T // tm
    n_pairs = d_attn // LANES
    kern = functools.partial(_in_proj_pool_kernel, tm=tm, tiles_per_seq=tiles_per_seq, n_tiles=n_tiles,
                             d_pool=d_pool, d_attn=d_attn)

    def proj_tile(g):
        return jnp.minimum(g, n_tiles - 1)

    def slab_spec(w, axis):
        shape = list(w.shape)
        shape[axis] //= n_tiles
        return pl.BlockSpec(tuple(shape), lambda g: tuple(proj_tile(g) if a == axis else 0 for a in range(2)))

    for w, axis in later_weights:
        assert w.ndim == 2 and w.shape[axis] % (n_tiles * BF16_SUBLANES) == 0, (w.shape, axis, n_tiles)
    slab_specs = [slab_spec(w, axis) for w, axis in later_weights]

    const2 = lambda g: (0, 0)
    qkv_spec = pl.BlockSpec((1, n_pairs, tm, LANES),
                            lambda g: (proj_tile(g) // tiles_per_seq, 0, proj_tile(g) % tiles_per_seq, 0))
    qkv_shape = jax.ShapeDtypeStruct((T // seq_len, n_pairs, seq_len, LANES), BF16)
    return pl.pallas_call(
        kern,
        grid=(n_tiles + 1,),
        in_specs=[
            pl.BlockSpec((tm, D), lambda g: (proj_tile(g), 0)),
            pl.BlockSpec((1, D), const2),
            pl.BlockSpec(w_in.shape, const2, pipeline_mode=pl.Buffered(1)),
            pl.BlockSpec(pool_w.shape, lambda g: (0, 0, 0)),
            pl.BlockSpec((1, d_pool), const2),
            pl.BlockSpec((1, d_pool), const2),
            slab_specs,
        ],
        out_specs=[pl.BlockSpec((tm, d_pool), lambda g: (jnp.maximum(g - 1, 0), 0)),
                   qkv_spec, qkv_spec, qkv_spec, slab_specs],
        out_shape=[jax.ShapeDtypeStruct((T, d_pool), BF16), qkv_shape, qkv_shape, qkv_shape,
                   [jax.ShapeDtypeStruct(w.shape, BF16) for w, _ in later_weights]],
        scratch_shapes=[pltpu.VMEM(w_in.shape, BF16)] +
                       [pltpu.VMEM((POOL_PAD + POOL_HALO + tm, d_pool - n * gdim), F32)
                        for n in range(n_groups)],
        compiler_params=pltpu.CompilerParams(
            dimension_semantics=("arbitrary",),
            vmem_limit_bytes=VMEM_LIMIT_BYTES),
        name="in_proj_pool",
    )(x2, norm1_g.reshape(1, D), w_in, pool_w, pool_scale.reshape(1, d_pool),
      pool_out_g.reshape(1, d_pool), [w for w, _ in later_weights])


def _stickbreak_kernel(q_ref, k_ref, v_ref, g_ref, o_ref, acc_ref, carry_ref, diag_ref, *, t, blocks_per_step):
    n_pairs = q_ref.shape[1]
    n_heads = LANES // HEAD_DIM
    rows = n_heads * t
    lane = lax.broadcasted_iota(jnp.int32, (t, LANES), 1)
    key_r = lax.broadcasted_iota(jnp.int32, (t, t), 0)
    key_c = lax.broadcasted_iota(jnp.int32, (t, t), 1)
    later_or_same = (key_r >= key_c).astype(BF16)
    qrow = lax.broadcasted_iota(jnp.int32, (rows, t), 0) & (t - 1)
    causal = lax.broadcasted_iota(jnp.int32, (rows, t), 1) < qrow
    half = t // 2
    late_old = (qrow >= half) & (lax.broadcasted_iota(jnp.int32, (rows, t), 1) < half)
    late_row = (lax.broadcasted_iota(jnp.int32, (rows, 1), 0) & (t - 1)) >= half

    def query_group(grp, _):
        first = grp * BLOCKS_PER_GROUP
        blk_i = [pl.program_id(1) * blocks_per_step + first + u for u in range(BLOCKS_PER_GROUP)]

        def q_rows(u):
            return pl.ds(pl.multiple_of((first + u) * t, t), t)

        def stacked_q(u, p):
            q2 = q_ref[0, p, q_rows(u), :]
            return jnp.concatenate(
                [jnp.where((lane >= h * HEAD_DIM) & (lane < (h + 1) * HEAD_DIM), q2, jnp.zeros_like(q2))
                 for h in range(n_heads)], axis=0)

        def pair_stages(p, qs, state, j0, j1, has_second, result):
            js = (j0, j1)
            zs, sps = [], []
            for n, j in enumerate(js):
                kb = k_ref[0, p, pl.ds(j * t, t), :]
                z = lax.dot_general(qs, kb, (((1,), (1,)), ((), ())), preferred_element_type=F32)
                if state is None and n == 0:
                    z = jnp.where(causal, z, MASKED_SCORE)
                if state is None and n == 1:
                    z = jnp.where(late_old, MASKED_SCORE, z)
                zs.append(z)
                sps.append(jnp.maximum(z, 0.0) + jnp.log2(1.0 + jnp.exp2(-jnp.abs(z))))
            yield
            sp_all = jnp.concatenate(sps, axis=0)
            from_here_all = jnp.dot(sp_all.astype(BF16), later_or_same, preferred_element_type=F32)
            yield
            pvs, block_sums = [], []
            for n, j in enumerate(js):
                from_here = from_here_all[n * rows:(n + 1) * rows]
                a = jnp.exp2(zs[n] - from_here)
                vb = v_ref[0, p, pl.ds(j * t, t), :]
                pvs.append(jnp.dot(a.astype(BF16), vb, preferred_element_type=F32))
                block_sums.append(from_here[:, 0:1])
            if state is None:
                carry, acc = block_sums[0], pvs[0]
            else:
                carry, acc = state
                acc = acc + jnp.exp2(-carry) * pvs[0]
                carry = carry + block_sums[0]
            if has_second is True:
                acc = acc + jnp.exp2(-carry) * pvs[1]
                carry = carry + block_sums[1]
            else:
                acc = acc + jnp.where(has_second, jnp.exp2(-carry), 0.0) * pvs[1]
                carry = carry + jnp.where(has_second, block_sums[1], 0.0)
            result.append((carry, acc, block_sums[0]))

        chains = [(u, p) for u in range(BLOCKS_PER_GROUP) for p in range(n_pairs)]
        results = [[] for _ in chains]
        stages = [pair_stages(p, stacked_q(u, p), None, blk_i[u], jnp.maximum(blk_i[u] - 1, 0),
                              True if u >= 1 else blk_i[u] >= 1, results[c])
                  for c, (u, p) in enumerate(chains)]
        n_stages = 3
        for tick in range(len(chains) + n_stages - 1):
            for c in reversed(range(len(chains))):
                if 0 <= tick - c < n_stages:
                    next(stages[c], None)

        needs_more = None
        needs_old_half = None
        for c, (u, p) in enumerate(chains):
            carry, acc, diag_sum = results[c][0]
            acc_ref[c] = acc
            carry_ref[c] = carry
            diag_ref[c] = diag_sum
            more = (blk_i[u] >= 2) & (jnp.min(carry) < EXIT_LOG2)
            needs_more = more if needs_more is None else needs_more | more
            late_min = jnp.min(jnp.where(late_row, carry, jnp.inf))
            old_half = (blk_i[u] >= 1) & (late_min < EXIT_LOG2)
            needs_old_half = old_half if needs_old_half is None else needs_old_half | old_half

        @pl.when(needs_old_half)
        def _():
            def per_chain(c, _):
                u = c // n_pairs
                p = c % n_pairs
                i = blk_i[0] + u
                valid = i >= 1
                j = jnp.maximum(i - 1, 0)
                qs = stacked_q(u, p)
                kb = k_ref[0, p, pl.ds(j * t, half), :]
                vb = v_ref[0, p, pl.ds(j * t, half), :]
                z = lax.dot_general(qs, kb, (((1,), (1,)), ((), ())), preferred_element_type=F32)
                sp = jnp.maximum(z, 0.0) + jnp.log2(1.0 + jnp.exp2(-jnp.abs(z)))
                sp = jnp.where(late_row, sp, 0.0)
                within = jnp.dot(sp.astype(BF16), later_or_same[0:half, 0:half], preferred_element_type=F32)
                carry = carry_ref[c]
                diag_sum = diag_ref[c]
                a = jnp.where(late_row, jnp.exp2(z - within - (carry - diag_sum)), 0.0)
                pv = jnp.dot(a.astype(BF16), vb, preferred_element_type=F32)
                acc_ref[c] = acc_ref[c] + jnp.where(valid, jnp.exp2(-diag_sum), 0.0) * pv
                carry_ref[c] = carry + jnp.where(valid & late_row, within[:, 0:1], 0.0)
                return 0

            lax.fori_loop(0, len(chains), per_chain, 0)

        @pl.when(needs_more)
        def _():
            def per_chain(c, _):
                u = c // n_pairs
                p = c % n_pairs
                i = blk_i[0] + u
                qs = stacked_q(u, p)

                def more_blocks(state):
                    n, chain_min, _, _ = state
                    return (n < i // 2) & (chain_min < EXIT_LOG2)

                def pair_body(state):
                    n, _, carry, acc = state
                    j0 = i - 2 - 2 * n
                    result = []
                    for _ in pair_stages(p, qs, (carry, acc), j0, jnp.maximum(j0 - 1, 0), j0 >= 1, result):
                        pass
                    carry, acc, _ = result[0]
                    return n + 1, jnp.min(carry), carry, acc

                carry = carry_ref[c]
                _, _, _, acc = lax.while_loop(more_blocks, pair_body,
                                              (0, jnp.min(carry), carry, acc_ref[c]))
                acc_ref[c] = acc
                return 0

            lax.fori_loop(0, len(chains), per_chain, 0)

        for u in range(BLOCKS_PER_GROUP):
            y = jnp.concatenate(
                [jnp.where(lane < HEAD_DIM, acc_ref[u * n_pairs + p, 0:t], acc_ref[u * n_pairs + p, t:2 * t])
                 for p in range(n_pairs)], axis=-1)
            o_ref[0, q_rows(u), :] = _rms(y, g_ref[...]).astype(BF16)
        return 0

    lax.fori_loop(0, blocks_per_step // BLOCKS_PER_GROUP, query_group, 0)


def _stickbreak(q, k, v, attn_out_g, *, t, blocks_per_step):
    B, n_pairs, S, _ = q.shape
    d_attn = n_pairs * LANES
    rows = (LANES // HEAD_DIM) * t
    ts = t * blocks_per_step
    kern = functools.partial(_stickbreak_kernel, t=t, blocks_per_step=blocks_per_step)
    full = lambda b, i: (b, 0, 0, 0)
    return pl.pallas_call(
        kern,
        grid=(B, S // ts),
        in_specs=[
            pl.BlockSpec((1, n_pairs, ts, LANES), lambda b, i: (b, 0, i, 0)),
            pl.BlockSpec((1, n_pairs, S, LANES), full),
            pl.BlockSpec((1, n_pairs, S, LANES), full),
            pl.BlockSpec((1, d_attn), lambda b, i: (0, 0)),
        ],
        out_specs=pl.BlockSpec((1, ts, d_attn), lambda b, i: (b, i, 0)),
        out_shape=jax.ShapeDtypeStruct((B, S, d_attn), BF16),
        scratch_shapes=[pltpu.VMEM((BLOCKS_PER_GROUP * n_pairs, rows, LANES), F32),
                        pltpu.VMEM((BLOCKS_PER_GROUP * n_pairs, rows, 1), F32),
                        pltpu.VMEM((BLOCKS_PER_GROUP * n_pairs, rows, 1), F32)],
        compiler_params=pltpu.CompilerParams(
            dimension_semantics=("arbitrary", "arbitrary"),
            vmem_limit_bytes=VMEM_LIMIT_BYTES),
        name="stickbreak",
    )(q, k, v, attn_out_g.reshape(1, d_attn))


def _out_mlp_kernel(x_ref, pool_ref, attn_ref, w_out_ref, g2_ref, w_up_ref, w_down_ref, gf_ref, o_ref, *,
                    ff_chunk):
    mixed = jnp.concatenate([pool_ref[...], attn_ref[...]], axis=-1)
    h = x_ref[...] + jnp.dot(mixed, w_out_ref[...], preferred_element_type=F32)

    hn2 = _rms(h, g2_ref[...]).astype(BF16)
    d_ff = w_up_ref.shape[1]
    acc = h
    for c in range(d_ff // ff_chunk):
        up = jnp.dot(hn2, w_up_ref[:, c * ff_chunk:(c + 1) * ff_chunk], preferred_element_type=F32)
        act = jnp.square(jnp.maximum(up, 0.0)).astype(BF16)
        acc = acc + jnp.dot(act, w_down_ref[c * ff_chunk:(c + 1) * ff_chunk, :],
                            preferred_element_type=F32)
    o_ref[...] = _rms(acc, gf_ref[...])


def _out_mlp(x2, pool_n, attn_n, w_out, norm2_g, w_up, w_down, final_g, *, tm, ff_chunk):
    T, D = x2.shape
    d_pool = pool_n.shape[1]
    d_attn = attn_n.shape[1]
    d_ff = w_up.shape[1]
    kern = functools.partial(_out_mlp_kernel, ff_chunk=ff_chunk)
    tok = lambda i: (i, 0)
    const = lambda i: (0, 0)
    resident = functools.partial(pl.BlockSpec, index_map=const, pipeline_mode=pl.Buffered(1))
    return pl.pallas_call(
        kern,
        grid=(T // tm,),
        in_specs=[
            pl.BlockSpec((tm, D), tok),
            pl.BlockSpec((tm, d_pool), tok),
            pl.BlockSpec((tm, d_attn), tok),
            resident((d_pool + d_attn, D)),
            pl.BlockSpec((1, D), const),
            resident((D, d_ff)),
            resident((d_ff, D)),
            pl.BlockSpec((1, D), const),
        ],
        out_specs=pl.BlockSpec((tm, D), tok),
        out_shape=jax.ShapeDtypeStruct((T, D), F32),
        compiler_params=pltpu.CompilerParams(
            dimension_semantics=("parallel",),
            vmem_limit_bytes=VMEM_LIMIT_BYTES),
        name="out_mlp",
    )(x2, pool_n, attn_n, w_out, norm2_g.reshape(1, D), w_up, w_down, final_g.reshape(1, D))


def kernel(x, norm1_g, w_in, pool_w, pool_scale, pool_out_g, attn_out_g, w_out, norm2_g, w_up, w_down,
           final_g):
    B, S, D = x.shape
    assert S % PROJ_ROWS == 0 and S % (ATTN_BLOCK * ATTN_BLOCKS_PER_STEP) == 0 and (B * S) % MLP_ROWS == 0
    assert w_up.shape[1] % FF_CHUNK == 0
    x2 = x.reshape(B * S, D)
    pool_n, q, k, v, (w_out_bf, w_up_bf, w_down_bf) = _in_proj_pool(
        x2, S, norm1_g, w_in, pool_w, pool_scale, pool_out_g, [(w_out, 0), (w_up, 1), (w_down, 0)],
        tm=PROJ_ROWS)
    attn_n = _stickbreak(q, k, v, attn_out_g, t=ATTN_BLOCK, blocks_per_step=ATTN_BLOCKS_PER_STEP)
    out = _out_mlp(x2, pool_n, attn_n.reshape(B * S, -1), w_out_bf, norm2_g, w_up_bf, w_down_bf, final_g,
                   tm=MLP_ROWS, ff_chunk=FF_CHUNK)
    return out.reshape(B, S, D)
```
